```python
import math
import jax, jax.numpy as jnp
from jax import lax
import numpy as np

D_MODEL = 4096
BATCH = 4
SEQ = 2048
DEPTH = 2
DEC_BATCH = 8
DEC_SEQ = 1
PAST_LEN = 16384
PAGE_SIZE = 128

HEAD_DIM = 128
MIX_WIDTH = D_MODEL
N_MIX_HEADS = MIX_WIDTH // HEAD_DIM
A_GROUPS = N_MIX_HEADS // 4
B_HEADS = (N_MIX_HEADS - A_GROUPS) // 2
C_HEADS = N_MIX_HEADS - A_GROUPS - B_HEADS
A_W = A_GROUPS * HEAD_DIM
B_W = B_HEADS * HEAD_DIM
C_W = C_HEADS * HEAD_DIM
CHUNK_MLP = 128
GDN_DK = HEAD_DIM
GDN_DV = HEAD_DIM
GDN_CHUNK = 64
CONV_WIDTH = 4
C_QK_DIM = HEAD_DIM // 2
C_V_DIM = HEAD_DIM
ATTN_Q_BLOCK = 128
EPS = 1e-6
SPLIT_SIZES = (A_W, A_W, A_W, 3 * B_W, B_W, B_HEADS, B_HEADS, C_W, C_W, C_W, C_W)
IN_COLS = 3 * A_W + 4 * B_W + 2 * B_HEADS + 4 * C_W

kernel_name = 'hymba_chunkmlp_gdn_diffattn_step'


def split_points():
    pts, acc = [], 0
    for s in SPLIT_SIZES[:-1]:
        acc += s
        pts.append(acc)
    return pts


def rmsnorm(x, gain):
    xf = x.astype(jnp.float32)
    y = xf * lax.rsqrt(jnp.mean(xf * xf, axis=-1, keepdims=True) + EPS)
    return (y * gain.astype(jnp.float32)).astype(x.dtype)


def l2norm(x):
    xf = x.astype(jnp.float32)
    return xf * lax.rsqrt(jnp.sum(xf * xf, axis=-1, keepdims=True) + EPS)


def alibi_slopes(n):
    def pow2(m):
        start = 2.0 ** (-8.0 / m)
        return [start ** (i + 1) for i in range(m)]
    p = 2 ** int(math.floor(math.log2(n)))
    s = pow2(p)
    if p < n:
        s = s + pow2(2 * p)[0::2][: n - p]
    return jnp.asarray(np.array(s, dtype=np.float32))


def chunk_spatial_gating(u, v, w_s, b_s):
    bn, t, _ = v.shape
    c = min(CHUNK_MLP, t)
    n = t // c
    mask = jnp.tril(jnp.ones((c, c), bool))
    w = jnp.where(mask, w_s[:, :c, :c], 0.0)
    vc = v.reshape(bn, n, c, A_GROUPS, HEAD_DIM)
    mixed = jnp.einsum('gts,bnsgd->bntgd', w.astype(v.dtype), vc)
    mixed = mixed + b_s[:, :c].T[None, None, :, :, None].astype(v.dtype)
    return u * mixed.reshape(bn, t, A_W)


def causal_short_conv(x, buf, w):
    t = x.shape[1]
    xp = jnp.concatenate([buf.astype(x.dtype), x], axis=1)
    out = xp[:, 0:t] * w[0]
    for j in range(1, CONV_WIDTH):
        out = out + xp[:, j:j + t] * w[j]
    return jax.nn.silu(out), xp[:, -(CONV_WIDTH - 1):]


def gated_delta_chunked(q, k, v, g, beta, s0):
    f32 = jnp.float32
    bn, t, h, dk = q.shape
    dv = v.shape[-1]
    c = min(GDN_CHUNK, t)
    pad = (-t) % c
    n = (t + pad) // c

    def prep(a):
        a = a.astype(f32)
        if pad:
            a = jnp.pad(a, [(0, 0), (0, pad)] + [(0, 0)] * (a.ndim - 2))
        a = a.reshape(bn, n, c, *a.shape[2:])
        return jnp.moveaxis(a, 3, 1)

    qc = prep(q) * (dk ** -0.5)
    kc, vc, gc, bc = prep(k), prep(v), prep(g), prep(beta)
    gcum = jnp.cumsum(gc, axis=-1)
    incl = jnp.tril(jnp.ones((c, c), bool))
    strict = jnp.tril(jnp.ones((c, c), bool), -1)
    diff = gcum[..., :, None] - gcum[..., None, :]
    decay = jnp.where(incl, jnp.exp(jnp.where(incl, diff, 0.0)), 0.0)
    kb = kc * bc[..., None]
    lmat = jnp.where(strict, jnp.einsum('bhnid,bhnjd->bhnij', kb, kc) * decay, 0.0)
    eye = jnp.eye(c, dtype=f32)
    tinv = lax.linalg.triangular_solve(eye + lmat, jnp.broadcast_to(eye, lmat.shape),
                                       left_side=True, lower=True, unit_diagonal=True)
    u = tinv @ (vc * bc[..., None])
    w = tinv @ (kb * jnp.exp(gcum)[..., None])
    a_intra = jnp.einsum('bhnid,bhnjd->bhnij', qc, kc) * decay
    q_dec = qc * jnp.exp(gcum)[..., None]
    k_dec = kc * jnp.exp(gcum[..., -1:] - gcum)[..., None]
    g_tot = jnp.exp(gcum[..., -1])

    def step(s, xs):
        u_i, w_i, a_i, qd_i, kd_i, gt_i = xs
        v_new = u_i - w_i @ s
        o_i = qd_i @ s + a_i @ v_new
        s = s * gt_i[..., None, None] + jnp.swapaxes(kd_i, -1, -2) @ v_new
        return s, o_i

    xs = tuple(jnp.moveaxis(a, 2, 0) for a in (u, w, a_intra, q_dec, k_dec, g_tot))
    s_fin, o = lax.scan(step, s0.astype(f32), xs)
    o = jnp.transpose(o, (1, 0, 3, 2, 4)).reshape(bn, n * c, h, dv)[:, :t]
    return o, s_fin


def alibi_scores(q, k, q_pos, k_pos, slopes):
    s = jnp.einsum('bqhcd,bkhcd->bhcqk', q.astype(jnp.float32), k.astype(jnp.float32)) * (C_QK_DIM ** -0.5)
    dist = (q_pos[:, None] - k_pos[None, :]).astype(jnp.float32)
    s = s - slopes[None, :, None, None, None] * dist
    return jnp.where(dist >= 0, s, -jnp.inf)


def diff_probs(s, lam):
    p = jax.nn.softmax(s, axis=-1)
    return p[:, :, 0] - lam * p[:, :, 1]


def prompt_diff_attention(q, k, v, q_pos, slopes, lam):
    bn, t = q.shape[:2]
    blk = min(ATTN_Q_BLOCK, t)
    nb = t // blk
    qb = jnp.moveaxis(q.reshape(bn, nb, blk, *q.shape[2:]), 1, 0)
    pb = q_pos.reshape(nb, blk)
    vf = v.astype(jnp.float32)

    def one(args):
        qi, pi = args
        pd = diff_probs(alibi_scores(qi, k, pi, q_pos, slopes), lam)
        return jnp.einsum('bhqk,bkhd->bqhd', pd, vf)

    o = lax.map(one, (qb, pb))
    return jnp.moveaxis(o, 0, 1).reshape(bn, t, C_HEADS, C_V_DIM)


def cached_diff_attention(q, k_new, v_new, k_past, v_past, q_pos, slopes, lam):
    past = k_past.shape[1]
    t = q.shape[1]
    s_past = alibi_scores(q, k_past, q_pos, jnp.arange(past, dtype=jnp.int32), slopes)
    s_new = alibi_scores(q, k_new, q_pos, past + jnp.arange(t, dtype=jnp.int32), slopes)
    pd = diff_probs(jnp.concatenate([s_past, s_new], axis=-1), lam)
    return (jnp.einsum('bhqk,bkhd->bqhd', pd[..., :past], v_past.astype(jnp.float32))
            + jnp.einsum('bhqk,bkhd->bqhd', pd[..., past:], v_new.astype(jnp.float32)))


def mixer_layer(x, lp, layer_idx, slopes, pos0, gdn_s0, conv_buf, k_past, v_past):
    f32 = jnp.float32
    bn, t, _ = x.shape
    h = rmsnorm(x, lp['norm_gain'])
    proj = jnp.einsum('btd,dc->btc', h, lp['w_in'])
    a_u, a_v, a_z, b_qkv, b_z, b_a, b_b, c_q, c_k, c_v, c_z = jnp.split(proj, split_points(), axis=-1)

    a_v = jax.nn.gelu(a_v, approximate=False)
    a_out = chunk_spatial_gating(jax.nn.gelu(a_u, approximate=False), a_v,
                                 lp['chunk_w'], lp['chunk_b']) * jax.nn.silu(a_z)

    b_act, conv_new = causal_short_conv(b_qkv, conv_buf, lp['gdn_conv_w'])
    bq, bk, bv = jnp.split(b_act, 3, axis=-1)
    bq = l2norm(bq.reshape(bn, t, B_HEADS, GDN_DK))
    bk = l2norm(bk.reshape(bn, t, B_HEADS, GDN_DK))
    bv = bv.reshape(bn, t, B_HEADS, GDN_DV)
    g = -jnp.exp(lp['gdn_a_log'].astype(f32)) * jax.nn.softplus(b_a.astype(f32) + lp['gdn_dt_bias'].astype(f32))
    beta = jax.nn.sigmoid(b_b.astype(f32))
    o_b, s_new = gated_delta_chunked(bq, bk, bv, g, beta, gdn_s0)
    b_out = rmsnorm(o_b, lp['gdn_norm_gain']).reshape(bn, t, B_W).astype(x.dtype) * jax.nn.silu(b_z)

    cq = rmsnorm(c_q.reshape(bn, t, C_HEADS, 2, C_QK_DIM), lp['attn_q_norm'])
    ck = rmsnorm(c_k.reshape(bn, t, C_HEADS, 2, C_QK_DIM), lp['attn_k_norm'])
    cv = c_v.reshape(bn, t, C_HEADS, C_V_DIM)
    lam_init = 0.8 - 0.6 * math.exp(-0.3 * layer_idx)
    lam = (jnp.exp(jnp.sum(lp['lambda_q1'].astype(f32) * lp['lambda_k1'].astype(f32)))
           - jnp.exp(jnp.sum(lp['lambda_q2'].astype(f32) * lp['lambda_k2'].astype(f32))) + lam_init)
    q_pos = pos0 + jnp.arange(t, dtype=jnp.int32)
    if k_past is None:
        o_c = prompt_diff_attention(cq, ck, cv, q_pos, slopes, lam)
    else:
        o_c = cached_diff_attention(cq, ck, cv, k_past, v_past, q_pos, slopes, lam)
    c_out = (rmsnorm(o_c, lp['attn_subln_gain']) * (1.0 - lam_init)).reshape(bn, t, C_W).astype(x.dtype)
    c_out = c_out * jax.nn.silu(c_z)

    mix = jnp.concatenate([a_out, b_out, c_out], axis=-1)
    y = x + jnp.einsum('btc,cd->btd', mix, lp['w_out']).astype(x.dtype)
    k_rows = ck.reshape(bn, t, C_HEADS, HEAD_DIM)
    return y, k_rows, cv, s_new, conv_new, a_v


def setup_inputs(seed: int = 0) -> dict:
    key = jax.random.key(seed)
    ks = jax.random.split(key, 24)
    f32 = jnp.float32
    n_pages = PAST_LEN // PAGE_SIZE
    used = DEC_BATCH * n_pages
    pool = used + (used + 3) // 4
    nrm = lambda k, shape, s: jax.random.normal(k, shape, f32) * s
    page_table = jax.random.permutation(ks[6], pool)[:used].reshape(DEC_BATCH, n_pages).astype(jnp.int32)
    dt = jnp.exp(jax.random.uniform(ks[13], (DEPTH, B_HEADS), f32) * (math.log(0.1) - math.log(1e-3)) + math.log(1e-3))
    return {
        'x_prompt': nrm(ks[0], (BATCH, SEQ, D_MODEL), 1.0),
        'x_sample': nrm(ks[1], (DEC_BATCH, DEC_SEQ, D_MODEL), 1.0),
        'cache_attn_k': nrm(ks[2], (DEPTH, pool, PAGE_SIZE, C_HEADS, HEAD_DIM), 1.0),
        'cache_attn_v': nrm(ks[3], (DEPTH, pool, PAGE_SIZE, C_HEADS, HEAD_DIM), 1.0),
        'state_gdn': nrm(ks[4], (DEPTH, DEC_BATCH, B_HEADS, GDN_DK, GDN_DV), 0.1),
        'state_gdn_conv': nrm(ks[5], (DEPTH, DEC_BATCH, CONV_WIDTH - 1, 3 * B_W), 1.0),
        'page_table': page_table,
        'norm_gain': 1.0 + nrm(ks[7], (DEPTH, D_MODEL), 0.02),
        'w_in': nrm(ks[8], (DEPTH, D_MODEL, IN_COLS), D_MODEL ** -0.5),
        'w_out': nrm(ks[9], (DEPTH, MIX_WIDTH, D_MODEL), MIX_WIDTH ** -0.5),
        'chunk_w': nrm(ks[10], (DEPTH, A_GROUPS, CHUNK_MLP, CHUNK_MLP), CHUNK_MLP ** -0.5),
        'chunk_b': nrm(ks[11], (DEPTH, A_GROUPS, CHUNK_MLP), 0.1),
        'gdn_conv_w': nrm(ks[12], (DEPTH, CONV_WIDTH, 3 * B_W), CONV_WIDTH ** -0.5),
        'gdn_a_log': jnp.log(jax.random.uniform(ks[14], (DEPTH, B_HEADS), f32, 1.0, 16.0)),
        'gdn_dt_bias': dt + jnp.log(-jnp.expm1(-dt)),
        'gdn_norm_gain': 1.0 + nrm(ks[15], (DEPTH, GDN_DV), 0.02),
        'attn_q_norm': 1.0 + nrm(ks[16], (DEPTH, C_QK_DIM), 0.02),
        'attn_k_norm': 1.0 + nrm(ks[17], (DEPTH, C_QK_DIM), 0.02),
        'lambda_q1': nrm(ks[18], (DEPTH, C_QK_DIM), 0.1),
        'lambda_k1': nrm(ks[19], (DEPTH, C_QK_DIM), 0.1),
        'lambda_q2': nrm(ks[20], (DEPTH, C_QK_DIM), 0.1),
        'lambda_k2': nrm(ks[21], (DEPTH, C_QK_DIM), 0.1),
        'attn_subln_gain': 1.0 + nrm(ks[22], (DEPTH, C_V_DIM), 0.02),
    }


def reference(x_prompt, x_sample, cache_attn_k, cache_attn_v, state_gdn, state_gdn_conv, page_table,
              norm_gain, w_in, w_out, chunk_w, chunk_b, gdn_conv_w, gdn_a_log, gdn_dt_bias, gdn_norm_gain,
              attn_q_norm, attn_k_norm, lambda_q1, lambda_k1, lambda_q2, lambda_k2, attn_subln_gain):
    past_len = page_table.shape[1] * PAGE_SIZE
    n_db = x_sample.shape[0]
    n_pb = x_prompt.shape[0]
    slopes = alibi_slopes(C_HEADS)
    yp, ys = x_prompt, x_sample
    pk_l, pv_l, ps_l, pc_l = [], [], [], []
    sk_l, sv_l, ss_l, sc_l, sa_l = [], [], [], [], []
    for l in range(DEPTH):
        lp = {'norm_gain': norm_gain[l], 'w_in': w_in[l], 'w_out': w_out[l],
              'chunk_w': chunk_w[l], 'chunk_b': chunk_b[l], 'gdn_conv_w': gdn_conv_w[l],
              'gdn_a_log': gdn_a_log[l], 'gdn_dt_bias': gdn_dt_bias[l], 'gdn_norm_gain': gdn_norm_gain[l],
              'attn_q_norm': attn_q_norm[l], 'attn_k_norm': attn_k_norm[l],
              'lambda_q1': lambda_q1[l], 'lambda_k1': lambda_k1[l],
              'lambda_q2': lambda_q2[l], 'lambda_k2': lambda_k2[l], 'attn_subln_gain': attn_subln_gain[l]}
        s0 = jnp.zeros((n_pb, B_HEADS, GDN_DK, GDN_DV), jnp.float32)
        buf0 = jnp.zeros((n_pb, CONV_WIDTH - 1, 3 * B_W), x_prompt.dtype)
        yp, pk, pv, ps, pc, _ = mixer_layer(yp, lp, l, slopes, 0, s0, buf0, None, None)
        k_past = cache_attn_k[l][page_table].reshape(n_db, past_len, C_HEADS, 2, C_QK_DIM)
        v_past = cache_attn_v[l][page_table].reshape(n_db, past_len, C_HEADS, C_V_DIM)
        ys, sk, sv, ss, sc, sa = mixer_layer(ys, lp, l, slopes, past_len, state_gdn[l], state_gdn_conv[l],
                                             k_past, v_past)
        pk_l.append(pk); pv_l.append(pv); ps_l.append(ps); pc_l.append(pc)
        sk_l.append(sk); sv_l.append(sv); ss_l.append(ss); sc_l.append(sc); sa_l.append(sa)
    return (yp, ys,
            jnp.stack(pk_l), jnp.stack(pv_l), jnp.stack(ps_l), jnp.stack(pc_l),
            jnp.stack(sk_l), jnp.stack(sv_l), jnp.stack(ss_l), jnp.stack(sc_l), jnp.stack(sa_l))
```

```python
import functools
import math

import jax
import jax.numpy as jnp
import numpy as np
from jax import lax
from jax.experimental import pallas as pl
from jax.experimental.pallas import tpu as pltpu

F32 = jnp.float32
BF16 = jnp.bfloat16
HIGHEST = lax.Precision.HIGHEST

HEAD_DIM = 128
A_GROUPS = 8
B_HEADS = 12
C_HEADS = 12
A_W = A_GROUPS * HEAD_DIM
B_W = B_HEADS * HEAD_DIM
C_W = C_HEADS * HEAD_DIM
QK_DIM = HEAD_DIM // 2
CHUNK_MLP = 128
GDN_CHUNK = 64
CONV_WIDTH = 4
EPS = 1e-6
INV_SQRT2 = 0.7071067811865476
NEG_BIG = -1e30

OFF_AU, OFF_AV, OFF_AZ = 0, A_W, 2 * A_W
OFF_BQ = 3 * A_W
OFF_BK = OFF_BQ + B_W
OFF_BV = OFF_BK + B_W
OFF_BZ = OFF_BV + B_W
OFF_CQ = OFF_BZ + B_W
OFF_CK = OFF_CQ + C_W
OFF_CV = OFF_CK + C_W
OFF_CZ = OFF_CV + C_W
MAIN_COLS = OFF_CZ + C_W
GATE_SRC = 3 * A_W + 4 * B_W
GATE_PAD = 128

VMEM_LIMIT = 56 * 1024 * 1024
PAGES_PER_STEP = 4
SCORE_ROWS = 8


def _cparams(n_axes):
    return pltpu.CompilerParams(dimension_semantics=("arbitrary",) * n_axes,
                                vmem_limit_bytes=VMEM_LIMIT)


def _gelu(x):
    return 0.5 * x * (1.0 + lax.erf(x * INV_SQRT2))


def _silu(x):
    return x * jax.nn.sigmoid(x)


def _dot(a, b):
    return jnp.dot(a, b, preferred_element_type=F32)


def _dot_hi(a, b):
    return jnp.dot(a, b, preferred_element_type=F32, precision=HIGHEST)


def _dot_nt(a, b):
    return lax.dot_general(a, b, (((1,), (1,)), ((), ())), preferred_element_type=F32)


def _dot_tn(a, b, precision=None):
    return lax.dot_general(a, b, (((0,), (0,)), ((), ())), preferred_element_type=F32,
                           precision=precision)


def _bdot_nt(a, b):
    return lax.dot_general(a, b, (((2,), (2,)), ((0,), (0,))), preferred_element_type=F32)


def _bdot(a, b):
    return lax.dot_general(a, b, (((2,), (1,)), ((0,), (0,))), preferred_element_type=F32)


def _alibi_slopes(n):
    def pow2(m):
        start = 2.0 ** (-8.0 / m)
        return [start ** (i + 1) for i in range(m)]
    p = 2 ** int(math.floor(math.log2(n)))
    s = pow2(p)
    if p < n:
        s = s + pow2(2 * p)[0::2][: n - p]
    return np.array(s, dtype=np.float32)


def _rmsnorm_kernel(x_ref, g_ref, o_ref):
    x = x_ref[...]
    ms = jnp.mean(x * x, axis=-1, keepdims=True)
    o_ref[...] = (x * lax.rsqrt(ms + EPS) * g_ref[...]).astype(o_ref.dtype)


def _rmsnorm(x, gain, tm):
    m, d = x.shape
    return pl.pallas_call(
        _rmsnorm_kernel,
        grid=(m // tm,),
        in_specs=[pl.BlockSpec((tm, d), lambda i: (i, 0)),
                  pl.BlockSpec((1, d), lambda i: (0, 0))],
        out_specs=pl.BlockSpec((tm, d), lambda i: (i, 0)),
        out_shape=jax.ShapeDtypeStruct((m, d), BF16),
        compiler_params=_cparams(1),
        name="rmsnorm",
    )(x, gain.reshape(1, d))


def _mm_kernel(a_ref, w_ref, o_ref):
    o_ref[...] = _dot(a_ref[...], w_ref[...])


def _matmul(a, w, tm, tn):
    m, k = a.shape
    n = w.shape[1]
    return pl.pallas_call(
        _mm_kernel,
        grid=(n // tn, m // tm),
        in_specs=[pl.BlockSpec((tm, k), lambda j, i: (i, 0)),
                  pl.BlockSpec((k, tn), lambda j, i: (0, j))],
        out_specs=pl.BlockSpec((tm, tn), lambda j, i: (i, j)),
        out_shape=jax.ShapeDtypeStruct((m, n), F32),
        compiler_params=_cparams(2),
        name="in_proj",
    )(a, w)


def _norm_mm_kernel(x_ref, g_ref, w_ref, o_ref):
    x = x_ref[...]
    ms = jnp.mean(x * x, axis=-1, keepdims=True)
    h = (x * lax.rsqrt(ms + EPS) * g_ref[...]).astype(BF16)
    o_ref[...] = _dot(h, w_ref[...])


def _norm_matmul_small(x, gain, w, tn):
    m, k = x.shape
    n = w.shape[1]
    return pl.pallas_call(
        _norm_mm_kernel,
        grid=(n // tn,),
        in_specs=[pl.BlockSpec((m, k), lambda j: (0, 0)),
                  pl.BlockSpec((1, k), lambda j: (0, 0)),
                  pl.BlockSpec((k, tn), lambda j: (0, j))],
        out_specs=pl.BlockSpec((m, tn), lambda j: (0, j)),
        out_shape=jax.ShapeDtypeStruct((m, n), F32),
        compiler_params=_cparams(1),
        name="in_proj_sample",
    )(x, gain.reshape(1, k), w)


def _outproj_kernel(x_ref, a_ref, b_ref, c_ref, wa_ref, wb_ref, wc_ref, o_ref):
    acc = _dot(a_ref[...].astype(BF16), wa_ref[...])
    acc = acc + _dot(b_ref[...].astype(BF16), wb_ref[...])
    acc = acc + _dot(c_ref[...].astype(BF16), wc_ref[...])
    o_ref[...] = x_ref[...] + acc


def _outproj(x, a, b, c, wa, wb, wc, tm, tn):
    m, d = x.shape
    return pl.pallas_call(
        _outproj_kernel,
        grid=(d // tn, m // tm),
        in_specs=[pl.BlockSpec((tm, tn), lambda j, i: (i, j)),
                  pl.BlockSpec((tm, a.shape[1]), lambda j, i: (i, 0)),
                  pl.BlockSpec((tm, b.shape[1]), lambda j, i: (i, 0)),
                  pl.BlockSpec((tm, c.shape[1]), lambda j, i: (i, 0)),
                  pl.BlockSpec((wa.shape[0], tn), lambda j, i: (0, j)),
                  pl.BlockSpec((wb.shape[0], tn), lambda j, i: (0, j)),
                  pl.BlockSpec((wc.shape[0], tn), lambda j, i: (0, j))],
        out_specs=pl.BlockSpec((tm, tn), lambda j, i: (i, j)),
        out_shape=jax.ShapeDtypeStruct((m, d), F32),
        compiler_params=_cparams(2),
        name="out_proj",
    )(x, a, b, c, wa, wb, wc)


def _gating_kernel(u_ref, v_ref, z_ref, w_ref, b_ref, o_ref):
    rows = u_ref.shape[0]
    ri = lax.broadcasted_iota(jnp.int32, (CHUNK_MLP, CHUNK_MLP), 0)
    ci = lax.broadcasted_iota(jnp.int32, (CHUNK_MLP, CHUNK_MLP), 1)
    w = jnp.where(ri >= ci, w_ref[0], 0.0).astype(BF16)
    bias = b_ref[0]
    for c in range(rows // CHUNK_MLP):
        sl = slice(c * CHUNK_MLP, (c + 1) * CHUNK_MLP)
        vg = _gelu(v_ref[sl, :]).astype(BF16)
        mixed = _dot(w, vg) + bias
        o_ref[sl, :] = (_gelu(u_ref[sl, :]) * mixed * _silu(z_ref[sl, :])).astype(o_ref.dtype)


def _gating_prompt(proj, chunk_w, bias_b, rows):
    m = proj.shape[0]
    nb = HEAD_DIM
    return pl.pallas_call(
        _gating_kernel,
        grid=(A_GROUPS, m // rows),
        in_specs=[pl.BlockSpec((rows, nb), lambda g, i: (i, OFF_AU // nb + g)),
                  pl.BlockSpec((rows, nb), lambda g, i: (i, OFF_AV // nb + g)),
                  pl.BlockSpec((rows, nb), lambda g, i: (i, OFF_AZ // nb + g)),
                  pl.BlockSpec((1, CHUNK_MLP, CHUNK_MLP), lambda g, i: (g, 0, 0)),
                  pl.BlockSpec((1, CHUNK_MLP, nb), lambda g, i: (g, 0, 0))],
        out_specs=pl.BlockSpec((rows, nb), lambda g, i: (i, g)),
        out_shape=jax.ShapeDtypeStruct((m, A_W), BF16),
        compiler_params=_cparams(2),
        name="gating_prompt",
    )(proj, proj, proj, chunk_w, bias_b)


def _gdn_prompt_kernel(q_ref, k_ref, v_ref, z_ref, ab_ref, wq_ref, wk_ref, wv_ref,
                       alog_ref, dtb_ref, gain_ref, o_ref, s_ref, st_ref):
    h = pl.program_id(1)
    t = q_ref.shape[1]
    pair = 2 * GDN_CHUNK
    lane = lax.broadcasted_iota(jnp.int32, (1, HEAD_DIM), 1)
    sel_a = lane == h
    sel_b = lane == (B_HEADS + h)
    neg_a = -jnp.exp(jnp.sum(jnp.where(sel_a, alog_ref[...], 0.0), axis=1, keepdims=True))
    dtb = jnp.sum(jnp.where(sel_a, dtb_ref[...], 0.0), axis=1, keepdims=True)

    ri = lax.broadcasted_iota(jnp.int32, (pair, pair), 0)
    ci = lax.broadcasted_iota(jnp.int32, (pair, pair), 1)
    same = (ri >= GDN_CHUNK) == (ci >= GDN_CHUNK)
    incl = same & (ri >= ci)
    strict = same & (ri > ci)
    incl_f = incl.astype(F32)
    eye = (ri == ci).astype(F32)
    row_lo = lax.broadcasted_iota(jnp.int32, (pair, HEAD_DIM), 0) < GDN_CHUNK
    gain = gain_ref[...]

    st_ref[...] = jnp.zeros_like(st_ref)

    def conv(x_ref, w_ref, i, r0, rp):
        cur = x_ref[0, pl.ds(r0, pair), :]
        prev = jnp.where(i > 0, x_ref[0, pl.ds(rp, 8), :], 0.0)
        win = jnp.concatenate([prev, cur], axis=0)
        w = w_ref[...]
        acc = win[5:5 + pair] * w[0:1]
        for j in range(1, CONV_WIDTH):
            acc = acc + win[5 + j:5 + j + pair] * w[j:j + 1]
        return _silu(acc)

    def body(i, carry):
        r0 = pl.multiple_of(i * pair, pair)
        rp = pl.multiple_of(jnp.maximum(r0 - 8, 0), 8)
        q = conv(q_ref, wq_ref, i, r0, rp)
        k = conv(k_ref, wk_ref, i, r0, rp)
        v = conv(v_ref, wv_ref, i, r0, rp)
        q = q * lax.rsqrt(jnp.sum(q * q, axis=-1, keepdims=True) + EPS) * (HEAD_DIM ** -0.5)
        k = k * lax.rsqrt(jnp.sum(k * k, axis=-1, keepdims=True) + EPS)
        ab = ab_ref[0, pl.ds(r0, pair), :]
        a_col = jnp.sum(jnp.where(sel_a, ab, 0.0), axis=1, keepdims=True)
        b_col = jnp.sum(jnp.where(sel_b, ab, 0.0), axis=1, keepdims=True)
        g_col = neg_a * jax.nn.softplus(a_col + dtb)
        beta = jax.nn.sigmoid(b_col)

        gcb = _dot_hi(incl_f, jnp.broadcast_to(g_col, (pair, HEAD_DIM)))
        egc = jnp.exp(gcb)
        glast = jnp.where(row_lo, gcb[GDN_CHUNK - 1:GDN_CHUNK, :], gcb[pair - 1:pair, :])
        kdec = k * jnp.exp(glast - gcb)
        decay = jnp.where(incl, jnp.exp(jnp.where(incl, gcb - gcb.T, 0.0)), 0.0)

        kb = k * beta
        k16 = k.astype(BF16)
        lmat = jnp.where(strict, _dot_nt(kb.astype(BF16), k16) * decay, 0.0)
        tinv = eye - lmat
        pw = _dot_hi(lmat, lmat)
        for step in range(5):
            tinv = tinv + _dot_hi(tinv, pw)
            if step < 4:
                pw = _dot_hi(pw, pw)
        t16 = tinv.astype(BF16)
        u = _dot(t16, (v * beta).astype(BF16))
        w = _dot(t16, (kb * egc).astype(BF16))
        a_intra = (_dot_nt(q.astype(BF16), k16) * decay).astype(BF16)
        qdec = (q * egc).astype(BF16)
        w16 = w.astype(BF16)

        s = st_ref[...]
        outs = []
        for c in range(2):
            sl = slice(c * GDN_CHUNK, (c + 1) * GDN_CHUNK)
            in_chunk = row_lo if c == 0 else jnp.logical_not(row_lo)
            s16 = s.astype(BF16)
            v_new = jnp.where(in_chunk, u - _dot(w16, s16), 0.0)
            vn16 = v_new.astype(BF16)
            o_c = _dot(qdec[sl], s16) + _dot(a_intra[sl], vn16)
            outs.append(o_c)
            gt = jnp.exp(gcb[(c + 1) * GDN_CHUNK - 1:(c + 1) * GDN_CHUNK, :])
            kd = jnp.where(in_chunk, kdec, 0.0).astype(BF16)
            s = s * gt + _dot_tn(kd, vn16)
        st_ref[...] = s
        o = jnp.concatenate(outs, axis=0)
        o = o * lax.rsqrt(jnp.mean(o * o, axis=-1, keepdims=True) + EPS) * gain
        o_ref[0, pl.ds(r0, pair), :] = (o * _silu(z_ref[0, pl.ds(r0, pair), :])).astype(o_ref.dtype)
        return carry

    lax.fori_loop(0, t // pair, body, 0)
    s_ref[0, 0] = st_ref[...]


def _gdn_prompt(proj3, ab3, conv_w, alog, dtb, gain):
    bn, t, _ = proj3.shape
    nb = HEAD_DIM
    col = lambda off: (lambda b, h: (b, 0, off // nb + h))
    wcol = lambda part: (lambda b, h: (0, part * B_HEADS + h))
    vec = pl.BlockSpec((1, nb), lambda b, h: (0, 0))
    return pl.pallas_call(
        _gdn_prompt_kernel,
        grid=(bn, B_HEADS),
        in_specs=[pl.BlockSpec((1, t, nb), col(OFF_BQ)),
                  pl.BlockSpec((1, t, nb), col(OFF_BK)),
                  pl.BlockSpec((1, t, nb), col(OFF_BV)),
                  pl.BlockSpec((1, t, nb), col(OFF_BZ)),
                  pl.BlockSpec((1, t, GATE_PAD), lambda b, h: (b, 0, 0)),
                  pl.BlockSpec((CONV_WIDTH, nb), wcol(0)),
                  pl.BlockSpec((CONV_WIDTH, nb), wcol(1)),
                  pl.BlockSpec((CONV_WIDTH, nb), wcol(2)),
                  vec, vec, vec],
        out_specs=[pl.BlockSpec((1, t, nb), lambda b, h: (b, 0, h)),
                   pl.BlockSpec((1, 1, nb, nb), lambda b, h: (b, h, 0, 0))],
        out_shape=[jax.ShapeDtypeStruct((bn, t, B_W), BF16),
                   jax.ShapeDtypeStruct((bn, B_HEADS, HEAD_DIM, HEAD_DIM), F32)],
        scratch_shapes=[pltpu.VMEM((HEAD_DIM, HEAD_DIM), F32)],
        compiler_params=_cparams(2),
        name="gdn_prompt",
    )(proj3, proj3, proj3, proj3, ab3, conv_w, conv_w, conv_w, alog, dtb, gain)


def _half_rmsnorm(x, gain2):
    lo = lax.broadcasted_iota(jnp.int32, x.shape, x.ndim - 1) < QK_DIM
    x2 = x * x
    s_lo = jnp.sum(jnp.where(lo, x2, 0.0), axis=-1, keepdims=True)
    s_hi = jnp.sum(jnp.where(lo, 0.0, x2), axis=-1, keepdims=True)
    ms = jnp.where(lo, s_lo, s_hi) * (1.0 / QK_DIM)
    return x * lax.rsqrt(ms + EPS) * gain2


def _cprep_kernel(q_ref, k_ref, v_ref, gq_ref, gk_ref, qb_ref, kn_ref, kb_ref, vo_ref, vb_ref):
    gq = gq_ref[...]
    gk = gk_ref[...]
    for h in range(C_HEADS):
        sl = slice(h * HEAD_DIM, (h + 1) * HEAD_DIM)
        qn = _half_rmsnorm(q_ref[0, :, sl], gq)
        kn = _half_rmsnorm(k_ref[0, :, sl], gk)
        qb_ref[0, :, sl] = (qn * (QK_DIM ** -0.5)).astype(BF16)
        kn_ref[0, :, sl] = kn
        kb_ref[0, :, sl] = kn.astype(BF16)
    v = v_ref[...]
    vo_ref[...] = v
    vb_ref[...] = v.astype(BF16)


def _cprep_prompt(proj3, gq2, gk2, tr):
    bn, t, _ = proj3.shape
    blk = lambda off: pl.BlockSpec((1, tr, C_W), lambda b, r: (b, r, off // C_W))
    out = pl.BlockSpec((1, tr, C_W), lambda b, r: (b, r, 0))
    vec = pl.BlockSpec((1, HEAD_DIM), lambda b, r: (0, 0))
    sd = lambda dt: jax.ShapeDtypeStruct((bn, t, C_W), dt)
    return pl.pallas_call(
        _cprep_kernel,
        grid=(bn, t // tr),
        in_specs=[blk(OFF_CQ), blk(OFF_CK), blk(OFF_CV), vec, vec],
        out_specs=[out, out, out, out, out],
        out_shape=[sd(BF16), sd(F32), sd(BF16), sd(F32), sd(BF16)],
        compiler_params=_cparams(2),
        name="cprep_prompt",
    )(proj3, proj3, proj3, gq2, gk2)


def _lambda_value(lamv, lam_init):
    e1 = jnp.exp(jnp.sum(lamv[0:1] * lamv[1:2], axis=-1, keepdims=True))
    e2 = jnp.exp(jnp.sum(lamv[2:3] * lamv[3:4], axis=-1, keepdims=True))
    return e1 - e2 + lam_init


def _attn_prompt_kernel(slopes_ref, q_ref, k_ref, v_ref, z_ref, lamv_ref, gain_ref, o_ref,
                        *, lam_init, tq, tk):
    h = pl.program_id(1)
    qi = pl.program_id(2)
    slope = slopes_ref[h]
    q = q_ref[0]
    lane = lax.broadcasted_iota(jnp.int32, q.shape, 1)
    zero = jnp.zeros_like(q)
    q1 = jnp.where(lane < QK_DIM, q, zero)
    q2 = jnp.where(lane < QK_DIM, zero, q)
    rowpos = qi * tq + lax.broadcasted_iota(jnp.int32, (tq, tk), 0)
    colpos = lax.broadcasted_iota(jnp.int32, (tq, tk), 1)

    def update(qc, kblk, vblk, valid, bias, m, l, acc):
        s = jnp.where(valid, _dot_nt(qc, kblk) - bias, NEG_BIG)
        m_new = jnp.maximum(m, jnp.max(s, axis=-1, keepdims=True))
        alpha = jnp.exp(m - m_new)
        p = jnp.exp(s - m_new)
        l = alpha * l + jnp.sum(p, axis=-1, keepdims=True)
        acc = alpha * acc + _dot(p.astype(BF16), vblk)
        return m_new, l, acc

    def body(j, carry):
        m1, l1, a1, m2, l2, a2 = carry
        c0 = pl.multiple_of(j * tk, tk)
        kblk = k_ref[0, pl.ds(c0, tk), :]
        vblk = v_ref[0, pl.ds(c0, tk), :]
        dist = (rowpos - (c0 + colpos)).astype(F32)
        valid = dist >= 0.0
        bias = slope * dist
        m1, l1, a1 = update(q1, kblk, vblk, valid, bias, m1, l1, a1)
        m2, l2, a2 = update(q2, kblk, vblk, valid, bias, m2, l2, a2)
        return m1, l1, a1, m2, l2, a2

    m0 = jnp.full((tq, 1), NEG_BIG, F32)
    l0 = jnp.zeros((tq, 1), F32)
    a0 = jnp.zeros((tq, HEAD_DIM), F32)
    n_kv = (qi * tq + tq + tk - 1) // tk
    m1, l1, a1, m2, l2, a2 = lax.fori_loop(0, n_kv, body, (m0, l0, a0, m0, l0, a0))
    lam = _lambda_value(lamv_ref[...], lam_init)
    o = a1 / l1 - lam * (a2 / l2)
    o = o * lax.rsqrt(jnp.mean(o * o, axis=-1, keepdims=True) + EPS) * gain_ref[...] * (1.0 - lam_init)
    o_ref[0] = (o * _silu(z_ref[0])).astype(o_ref.dtype)


def _attn_prompt(slopes, qb, kb, vb, proj3, lamv, gain, lam_init, tq):
    bn, t, _ = qb.shape
    nb = HEAD_DIM
    kern = functools.partial(_attn_prompt_kernel, lam_init=lam_init, tq=tq, tk=tq)
    return pl.pallas_call(
        kern,
        grid=(bn, C_HEADS, t // tq),
        in_specs=[pl.BlockSpec(memory_space=pltpu.SMEM),
                  pl.BlockSpec((1, tq, nb), lambda b, h, i: (b, i, h)),
                  pl.BlockSpec((1, t, nb), lambda b, h, i: (b, 0, h)),
                  pl.BlockSpec((1, t, nb), lambda b, h, i: (b, 0, h)),
                  pl.BlockSpec((1, tq, nb), lambda b, h, i: (b, i, OFF_CZ // nb + h)),
                  pl.BlockSpec((4, QK_DIM), lambda b, h, i: (0, 0)),
                  pl.BlockSpec((1, nb), lambda b, h, i: (0, 0))],
        out_specs=pl.BlockSpec((1, tq, nb), lambda b, h, i: (b, i, h)),
        out_shape=jax.ShapeDtypeStruct((bn, t, C_W), BF16),
        compiler_params=_cparams(3),
        name="attn_prompt",
    )(slopes, qb, kb, vb, proj3, lamv, gain)


def _sample_mix_kernel(p_ref, ab_ref, cbuf_ref, cw_ref, st_ref, w00_ref, b0_ref, alog_ref, dtb_ref,
                       ggain_ref, gq_ref, gk_ref,
                       aout_ref, av_ref, bout_ref, snew_ref, qn_ref, kn_ref, vn_ref, zc_ref):
    nb = HEAD_DIM
    p = p_ref[0]
    av = _gelu(p[:, OFF_AV:OFF_AV + A_W])
    av_ref[0] = av
    mixed = w00_ref[...] * av + b0_ref[...]
    aout_ref[0] = _gelu(p[:, OFF_AU:OFF_AU + A_W]) * mixed * _silu(p[:, OFF_AZ:OFF_AZ + A_W])

    cb = cbuf_ref[0]
    cw = cw_ref[...]
    x = p[:, OFF_BQ:OFF_BQ + 3 * B_W]
    acc = cb[0:1] * cw[0:1] + cb[1:2] * cw[1:2] + cb[2:3] * cw[2:3] + x * cw[3:4]
    act = _silu(acc)
    ab = ab_ref[0]
    g_row = -jnp.exp(alog_ref[...]) * jax.nn.softplus(ab + dtb_ref[...])
    beta_row = jax.nn.sigmoid(ab)
    ggain = ggain_ref[...]
    row8 = lax.broadcasted_iota(jnp.int32, (8, nb), 0)
    for h in range(B_HEADS):
        q = act[:, h * nb:(h + 1) * nb]
        k = act[:, B_W + h * nb:B_W + (h + 1) * nb]
        v = act[:, 2 * B_W + h * nb:2 * B_W + (h + 1) * nb]
        q = q * lax.rsqrt(jnp.sum(q * q, axis=-1, keepdims=True) + EPS) * (nb ** -0.5)
        k = k * lax.rsqrt(jnp.sum(k * k, axis=-1, keepdims=True) + EPS)
        eg = jnp.exp(g_row[:, h:h + 1])
        beta = beta_row[:, B_HEADS + h:B_HEADS + h + 1]
        s = st_ref[0, h]
        lhs = jnp.where(row8 == 0, k * (beta * eg), jnp.where(row8 == 1, q * eg, 0.0))
        rs = _dot_hi(lhs, s)
        v_new = v * beta - rs[0:1]
        o = rs[1:2] + jnp.sum(q * k, axis=-1, keepdims=True) * v_new
        k8 = jnp.where(row8 == 0, k, 0.0)
        v8 = jnp.where(row8 == 0, v_new, 0.0)
        snew_ref[0, h] = s * eg + _dot_tn(k8, v8, precision=HIGHEST)
        o = o * lax.rsqrt(jnp.mean(o * o, axis=-1, keepdims=True) + EPS) * ggain
        bout_ref[0, :, h * nb:(h + 1) * nb] = o * _silu(p[:, OFF_BZ + h * nb:OFF_BZ + (h + 1) * nb])

    gq = gq_ref[...]
    gk = gk_ref[...]
    for h in range(C_HEADS):
        qn_ref[0, h] = _half_rmsnorm(p[:, OFF_CQ + h * nb:OFF_CQ + (h + 1) * nb], gq) * (QK_DIM ** -0.5)
        kn_ref[0, h] = _half_rmsnorm(p[:, OFF_CK + h * nb:OFF_CK + (h + 1) * nb], gk)
        vn_ref[0, h] = p[:, OFF_CV + h * nb:OFF_CV + (h + 1) * nb]
        zc_ref[0, h] = p[:, OFF_CZ + h * nb:OFF_CZ + (h + 1) * nb]


def _sample_mix(proj3, ab3, cbuf, conv_w, state, w00, b0, alog, dtb, ggain, gq2, gk2):
    bn = proj3.shape[0]
    nb = HEAD_DIM
    row = lambda w: pl.BlockSpec((1, 1, w), lambda b: (b, 0, 0))
    full2 = lambda a: pl.BlockSpec(a.shape, lambda b: (0, 0))
    st = pl.BlockSpec((1, B_HEADS, nb, nb), lambda b: (b, 0, 0, 0))
    heads = pl.BlockSpec((1, C_HEADS, 1, nb), lambda b: (b, 0, 0, 0))
    sd = lambda w: jax.ShapeDtypeStruct((bn, 1, w), F32)
    hd = jax.ShapeDtypeStruct((bn, C_HEADS, 1, nb), F32)
    return pl.pallas_call(
        _sample_mix_kernel,
        grid=(bn,),
        in_specs=[row(MAIN_COLS), row(GATE_PAD),
                  pl.BlockSpec((1, CONV_WIDTH - 1, 3 * B_W), lambda b: (b, 0, 0)),
                  full2(conv_w), st, full2(w00), full2(b0), full2(alog), full2(dtb),
                  full2(ggain), full2(gq2), full2(gk2)],
        out_specs=[row(A_W), row(A_W), row(B_W), st, heads, heads, heads, heads],
        out_shape=[sd(A_W), sd(A_W), sd(B_W),
                   jax.ShapeDtypeStruct(state.shape, F32), hd, hd, hd, hd],
        compiler_params=_cparams(1),
        name="sample_mix",
    )(proj3, ab3, cbuf, conv_w, state, w00, b0, alog, dtb, ggain, gq2, gk2)


def _paged_attn_kernel(pt_ref, qn_ref, kn_ref, vn_ref, z_ref, slope_ref, lamv_ref, gain_ref, *rest,
                       lam_init, past_len, n_groups):
    del pt_ref
    pp = PAGES_PER_STEP
    k_refs = rest[:pp]
    v_refs = rest[pp:2 * pp]
    o_ref = rest[2 * pp]
    sc_ref, mrun_ref, lrun_ref, acc_ref, q3_ref, snew_ref = rest[2 * pp + 1:]
    ph = pl.program_id(1)
    g = pl.program_id(2)
    page = k_refs[0].shape[1]
    sshape = mrun_ref.shape

    @pl.when((ph == 0) & (g == 0))
    def _():
        q = jnp.broadcast_to(qn_ref[0], q3_ref.shape)
        ri = lax.broadcasted_iota(jnp.int32, q3_ref.shape, 1)
        ci = lax.broadcasted_iota(jnp.int32, q3_ref.shape, 2)
        q3_ref[...] = jnp.where((ci // QK_DIM) == ri, q, 0.0)
        mrun_ref[...] = jnp.full(sshape, NEG_BIG, F32)
        lrun_ref[...] = jnp.zeros(sshape, F32)
        acc_ref[...] = jnp.zeros(acc_ref.shape, F32)

    @pl.when(ph == 0)
    def _():
        q3 = q3_ref[...].astype(BF16)
        slope = slope_ref[...]
        tok = lax.broadcasted_iota(jnp.int32, sshape, 2)
        mrun = mrun_ref[...]
        for i in range(pp):
            pg = g * pp + i
            s = _bdot_nt(q3, k_refs[i][...].astype(BF16))
            dist = (past_len - (pg * page + tok)).astype(F32)
            s = s - slope * dist
            sc_ref[pg] = s
            mrun = jnp.maximum(mrun, s)
        mrun_ref[...] = mrun

    @pl.when((ph == 1) & (g == 0))
    def _():
        s_new = jnp.sum(q3_ref[...].astype(BF16).astype(F32) * kn_ref[0], axis=-1, keepdims=True)
        m = jnp.maximum(jnp.max(mrun_ref[...], axis=-1, keepdims=True), s_new)
        mrun_ref[...] = jnp.broadcast_to(m, sshape)
        snew_ref[...] = jnp.broadcast_to(s_new, sshape)

    @pl.when(ph == 1)
    def _():
        m = mrun_ref[...]
        lrun = lrun_ref[...]
        acc = acc_ref[...]
        for i in range(pp):
            pg = g * pp + i
            p = jnp.exp(sc_ref[pg] - m)
            lrun = lrun + p
            acc = acc + _bdot(p.astype(BF16), v_refs[i][...].astype(BF16))
        lrun_ref[...] = lrun
        acc_ref[...] = acc

    @pl.when((ph == 1) & (g == n_groups - 1))
    def _():
        m = mrun_ref[:, :, 0:1]
        p_new = jnp.exp(snew_ref[:, :, 0:1] - m)
        l = jnp.sum(lrun_ref[...], axis=-1, keepdims=True) + p_new
        normed = (acc_ref[...] + p_new * vn_ref[0]) / l
        lam = _lambda_value(lamv_ref[...], lam_init)
        o = normed[:, 0:1, :] - lam * normed[:, 1:2, :]
        o = o * lax.rsqrt(jnp.mean(o * o, axis=-1, keepdims=True) + EPS) * gain_ref[...] * (1.0 - lam_init)
        o_ref[0] = o * _silu(z_ref[0])


def _paged_attn(page_table, qn, kn, vn, zc, cache_k, cache_v, layer, slope3, lamv, gain, lam_init):
    bn, n_pages = page_table.shape
    page = cache_k.shape[3]
    pp = PAGES_PER_STEP
    assert n_pages % pp == 0
    n_groups = n_pages // pp
    past_len = n_pages * page
    head_row = pl.BlockSpec((1, C_HEADS, 1, HEAD_DIM), lambda b, ph, g, pt: (b, 0, 0, 0))

    def k_spec(i):
        def imap(b, ph, g, pt):
            grp = jnp.where(ph == 0, g, n_groups - 1)
            return (layer, pt[b, grp * pp + i], 0, 0, 0)
        return pl.BlockSpec((None, None, C_HEADS, page, HEAD_DIM), imap)

    def v_spec(i):
        def imap(b, ph, g, pt):
            grp = jnp.where(ph == 0, 0, g)
            return (layer, pt[b, grp * pp + i], 0, 0, 0)
        return pl.BlockSpec((None, None, C_HEADS, page, HEAD_DIM), imap)

    def full(a):
        nd = a.ndim
        return pl.BlockSpec(a.shape, lambda b, ph, g, pt: (0,) * nd)

    sshape = (C_HEADS, SCORE_ROWS, page)
    qshape = (C_HEADS, SCORE_ROWS, HEAD_DIM)
    kern = functools.partial(_paged_attn_kernel, lam_init=lam_init, past_len=past_len, n_groups=n_groups)
    grid_spec = pltpu.PrefetchScalarGridSpec(
        num_scalar_prefetch=1,
        grid=(bn, 2, n_groups),
        in_specs=[head_row, head_row, head_row, head_row, full(slope3), full(lamv), full(gain)]
                 + [k_spec(i) for i in range(pp)] + [v_spec(i) for i in range(pp)],
        out_specs=head_row,
        scratch_shapes=[pltpu.VMEM((n_pages,) + sshape, F32),
                        pltpu.VMEM(sshape, F32),
                        pltpu.VMEM(sshape, F32),
                        pltpu.VMEM(qshape, F32),
                        pltpu.VMEM(qshape, F32),
                        pltpu.VMEM(sshape, F32)])
    return pl.pallas_call(
        kern,
        grid_spec=grid_spec,
        out_shape=jax.ShapeDtypeStruct((bn, C_HEADS, 1, HEAD_DIM), F32),
        compiler_params=_cparams(3),
        name="paged_attn",
    )(page_table, qn, kn, vn, zc, slope3, lamv, gain, *([cache_k] * pp), *([cache_v] * pp))


def _pad_lanes(v, width=GATE_PAD):
    return jnp.pad(v.astype(F32), (0, width - v.shape[0])).reshape(1, width)


def _layer_params(l, norm_gain, w_in, w_out, chunk_w, chunk_b, gdn_conv_w, gdn_a_log, gdn_dt_bias,
                  gdn_norm_gain, attn_q_norm, attn_k_norm, lq1, lk1, lq2, lk2, subln):
    w = w_in[l]
    gate_end = GATE_SRC + 2 * B_HEADS
    w_main = jnp.concatenate([w[:, :GATE_SRC], w[:, gate_end:]], axis=1).astype(BF16)
    w_gate = jnp.pad(w[:, GATE_SRC:gate_end], ((0, 0), (0, GATE_PAD - 2 * B_HEADS))).astype(BF16)
    wo = w_out[l]
    return dict(
        norm_gain=norm_gain[l], w_main=w_main, w_gate=w_gate,
        wo_a=wo[:A_W].astype(BF16), wo_b=wo[A_W:A_W + B_W].astype(BF16), wo_c=wo[A_W + B_W:].astype(BF16),
        chunk_w=chunk_w[l],
        bias_b=jnp.broadcast_to(chunk_b[l][:, :, None], (A_GROUPS, CHUNK_MLP, HEAD_DIM)),
        w00=jnp.repeat(chunk_w[l][:, 0, 0], HEAD_DIM).reshape(1, A_W),
        b0=jnp.repeat(chunk_b[l][:, 0], HEAD_DIM).reshape(1, A_W),
        conv_w=gdn_conv_w[l], alog=_pad_lanes(gdn_a_log[l]), dtb=_pad_lanes(gdn_dt_bias[l]),
        ggain=gdn_norm_gain[l].reshape(1, HEAD_DIM),
        gq2=jnp.tile(attn_q_norm[l], 2).reshape(1, HEAD_DIM),
        gk2=jnp.tile(attn_k_norm[l], 2).reshape(1, HEAD_DIM),
        lamv=jnp.stack([lq1[l], lk1[l], lq2[l], lk2[l]]).astype(F32),
        subln=subln[l].reshape(1, HEAD_DIM),
        lam_init=0.8 - 0.6 * math.exp(-0.3 * l),
    )


def _prompt_layer(x, lp, slopes):
    bn, t, d = x.shape
    m = bn * t
    x2 = x.reshape(m, d)
    tm = min(512, m)
    h = _rmsnorm(x2, lp['norm_gain'], min(256, m))
    proj = _matmul(h, lp['w_main'], tm, 1024)
    gates = _matmul(h, lp['w_gate'], tm, GATE_PAD)
    proj3 = proj.reshape(bn, t, MAIN_COLS)
    gates3 = gates.reshape(bn, t, GATE_PAD)

    a_out = _gating_prompt(proj, lp['chunk_w'], lp['bias_b'], min(512, t))
    b_out, s_fin = _gdn_prompt(proj3, gates3, lp['conv_w'], lp['alog'], lp['dtb'], lp['ggain'])
    qb, kn, kb, vo, vb = _cprep_prompt(proj3, lp['gq2'], lp['gk2'], min(256, t))
    c_out = _attn_prompt(slopes, qb, kb, vb, proj3, lp['lamv'], lp['subln'], lp['lam_init'], min(256, t))

    y = _outproj(x2, a_out, b_out.reshape(m, B_W), c_out.reshape(m, C_W),
                 lp['wo_a'], lp['wo_b'], lp['wo_c'], tm, 1024)
    conv_new = proj3[:, t - (CONV_WIDTH - 1):, OFF_BQ:OFF_BQ + 3 * B_W]
    return (y.reshape(bn, t, d), kn.reshape(bn, t, C_HEADS, HEAD_DIM), vo.reshape(bn, t, C_HEADS, HEAD_DIM),
            s_fin, conv_new)


def _sample_layer(x, lp, layer, slope_col, state, cbuf, cache_k, cache_v, page_table):
    bn, t, d = x.shape
    x2 = x.reshape(bn, d)
    proj = _norm_matmul_small(x2, lp['norm_gain'], lp['w_main'], 1024)
    gates = _norm_matmul_small(x2, lp['norm_gain'], lp['w_gate'], GATE_PAD)
    proj3 = proj.reshape(bn, 1, MAIN_COLS)
    gates3 = gates.reshape(bn, 1, GATE_PAD)
    a_out, a_v, b_out, s_new, qn, kn, vn, zc = _sample_mix(
        proj3, gates3, cbuf, lp['conv_w'], state, lp['w00'], lp['b0'], lp['alog'], lp['dtb'],
        lp['ggain'], lp['gq2'], lp['gk2'])
    c_out = _paged_attn(page_table, qn, kn, vn, zc, cache_k, cache_v, layer, slope_col,
                        lp['lamv'], lp['subln'], lp['lam_init'])
    y = _outproj(x2, a_out.reshape(bn, A_W), b_out.reshape(bn, B_W), c_out.reshape(bn, C_W),
                 lp['wo_a'], lp['wo_b'], lp['wo_c'], bn, 1024)
    conv_new = jnp.concatenate([cbuf[:, 1:], proj3[:, :, OFF_BQ:OFF_BQ + 3 * B_W]], axis=1)
    return (y.reshape(bn, t, d), kn.reshape(bn, 1, C_HEADS, HEAD_DIM), vn.reshape(bn, 1, C_HEADS, HEAD_DIM),
            s_new, conv_new, a_v)


def kernel(x_prompt, x_sample, cache_attn_k, cache_attn_v, state_gdn, state_gdn_conv, page_table,
           norm_gain, w_in, w_out, chunk_w, chunk_b, gdn_conv_w, gdn_a_log, gdn_dt_bias, gdn_norm_gain,
           attn_q_norm, attn_k_norm, lambda_q1, lambda_k1, lambda_q2, lambda_k2, attn_subln_gain):
    depth = w_in.shape[0]
    slopes_np = _alibi_slopes(C_HEADS)
    slopes = jnp.asarray(slopes_np)
    page = cache_attn_k.shape[2]
    slope_col = jnp.asarray(np.broadcast_to(slopes_np[:, None, None], (C_HEADS, SCORE_ROWS, page)).copy())
    ck = jnp.transpose(cache_attn_k, (0, 1, 3, 2, 4))
    cv = jnp.transpose(cache_attn_v, (0, 1, 3, 2, 4))

    yp, ys = x_prompt, x_sample
    outs = [[] for _ in range(9)]
    for l in range(depth):
        lp = _layer_params(l, norm_gain, w_in, w_out, chunk_w, chunk_b, gdn_conv_w, gdn_a_log, gdn_dt_bias,
                           gdn_norm_gain, attn_q_norm, attn_k_norm, lambda_q1, lambda_k1, lambda_q2,
                           lambda_k2, attn_subln_gain)
        yp, pk, pv, ps, pc = _prompt_layer(yp, lp, slopes)
        ys, sk, sv, ss, sc, sa = _sample_layer(ys, lp, l, slope_col, state_gdn[l], state_gdn_conv[l],
                                               ck, cv, page_table)
        for lst, val in zip(outs, (pk, pv, ps, pc, sk, sv, ss, sc, sa)):
            lst.append(val)
    return (yp, ys) + tuple(jnp.stack(o) for o in outs)
```

```python
import functools
import math

import jax
import jax.numpy as jnp
import numpy as np
from jax import lax
from jax.experimental import pallas as pl
from jax.experimental.pallas import tpu as pltpu

F32 = jnp.float32
BF16 = jnp.bfloat16
HIGHEST = lax.Precision.HIGHEST

HEAD_DIM = 128
A_GROUPS = 8
B_HEADS = 12
C_HEADS = 12
A_W = A_GROUPS * HEAD_DIM
B_W = B_HEADS * HEAD_DIM
C_W = C_HEADS * HEAD_DIM
QK_DIM = HEAD_DIM // 2
CHUNK_MLP = 128
GDN_CHUNK = 64
CONV_WIDTH = 4
EPS = 1e-6
INV_SQRT2 = 0.7071067811865476
NEG_BIG = -1e30

OFF_AU, OFF_AV, OFF_AZ = 0, A_W, 2 * A_W
OFF_BQ = 3 * A_W
OFF_BK = OFF_BQ + B_W
OFF_BV = OFF_BK + B_W
OFF_BZ = OFF_BV + B_W
OFF_CQ = OFF_BZ + B_W
OFF_CK = OFF_CQ + C_W
OFF_CV = OFF_CK + C_W
OFF_CZ = OFF_CV + C_W
MAIN_COLS = OFF_CZ + C_W
GATE_SRC = 3 * A_W + 4 * B_W
GATE_PAD = 128

VMEM_LIMIT = 56 * 1024 * 1024
PAGES_PER_STEP = 4
SCORE_ROWS = 8


def _cparams(n_axes):
    return pltpu.CompilerParams(dimension_semantics=("arbitrary",) * n_axes,
                                vmem_limit_bytes=VMEM_LIMIT)


def _gelu(x):
    return 0.5 * x * (1.0 + lax.erf(x * INV_SQRT2))


def _silu(x):
    return x * jax.nn.sigmoid(x)


def _dot(a, b):
    return jnp.dot(a, b, preferred_element_type=F32)


def _dot_hi(a, b):
    return jnp.dot(a, b, preferred_element_type=F32, precision=HIGHEST)


def _dot_nt(a, b):
    return lax.dot_general(a, b, (((1,), (1,)), ((), ())), preferred_element_type=F32)


def _dot_tn(a, b, precision=None):
    return lax.dot_general(a, b, (((0,), (0,)), ((), ())), preferred_element_type=F32,
                           precision=precision)


def _bdot_nt(a, b):
    return lax.dot_general(a, b, (((2,), (2,)), ((0,), (0,))), preferred_element_type=F32)


def _bdot(a, b):
    return lax.dot_general(a, b, (((2,), (1,)), ((0,), (0,))), preferred_element_type=F32)


def _alibi_slopes(n):
    def pow2(m):
        start = 2.0 ** (-8.0 / m)
        return [start ** (i + 1) for i in range(m)]
    p = 2 ** int(math.floor(math.log2(n)))
    s = pow2(p)
    if p < n:
        s = s + pow2(2 * p)[0::2][: n - p]
    return np.array(s, dtype=np.float32)


def _rmsnorm_kernel(x_ref, g_ref, o_ref):
    x = x_ref[...]
    ms = jnp.mean(x * x, axis=-1, keepdims=True)
    o_ref[...] = (x * lax.rsqrt(ms + EPS) * g_ref[...]).astype(o_ref.dtype)


def _rmsnorm(x, gain, tm):
    m, d = x.shape
    return pl.pallas_call(
        _rmsnorm_kernel,
        grid=(m // tm,),
        in_specs=[pl.BlockSpec((tm, d), lambda i: (i, 0)),
                  pl.BlockSpec((1, d), lambda i: (0, 0))],
        out_specs=pl.BlockSpec((tm, d), lambda i: (i, 0)),
        out_shape=jax.ShapeDtypeStruct((m, d), BF16),
        compiler_params=_cparams(1),
        name="rmsnorm",
    )(x, gain.reshape(1, d))


def _main_row_start(j, tn):
    return pl.multiple_of(j * tn + jnp.where(j * tn >= GATE_SRC, 2 * B_HEADS, 0), 8)


def _wt_spec(layer, tn, k, row_start, n_grid_axes):
    if n_grid_axes == 2:
        imap = lambda j, i: (layer, row_start(j), 0)
    else:
        imap = lambda j: (layer, row_start(j), 0)
    return pl.BlockSpec((pl.Element(1), pl.Element(tn), pl.Element(k)), imap)


def _inproj_kernel(a_ref, w_ref, o_ref, wt_ref):
    @pl.when(pl.program_id(1) == 0)
    def _():
        wt_ref[...] = w_ref[0].T.astype(BF16)
    o_ref[...] = _dot(a_ref[...], wt_ref[...])


def _inproj(a, w_t, layer, row_start, n_out, tm, tn):
    m, k = a.shape
    return pl.pallas_call(
        _inproj_kernel,
        grid=(n_out // tn, m // tm),
        in_specs=[pl.BlockSpec((tm, k), lambda j, i: (i, 0)),
                  _wt_spec(layer, tn, k, row_start, 2)],
        out_specs=pl.BlockSpec((tm, tn), lambda j, i: (i, j)),
        out_shape=jax.ShapeDtypeStruct((m, n_out), F32),
        scratch_shapes=[pltpu.VMEM((k, tn), BF16)],
        compiler_params=_cparams(2),
        name="in_proj",
    )(a, w_t)


def _norm_inproj_kernel(x_ref, g_ref, w_ref, o_ref):
    x = x_ref[...]
    ms = jnp.mean(x * x, axis=-1, keepdims=True)
    h = (x * lax.rsqrt(ms + EPS) * g_ref[...]).astype(BF16)
    o_ref[...] = _dot_nt(h, w_ref[0].astype(BF16))


def _norm_inproj_small(x, gain, w_t, layer, row_start, n_out, tn):
    m, k = x.shape
    return pl.pallas_call(
        _norm_inproj_kernel,
        grid=(n_out // tn,),
        in_specs=[pl.BlockSpec((m, k), lambda j: (0, 0)),
                  pl.BlockSpec((1, k), lambda j: (0, 0)),
                  _wt_spec(layer, tn, k, row_start, 1)],
        out_specs=pl.BlockSpec((m, tn), lambda j: (0, j)),
        out_shape=jax.ShapeDtypeStruct((m, n_out), F32),
        compiler_params=_cparams(1),
        name="in_proj_sample",
    )(x, gain.reshape(1, k), w_t)


def _outproj_kernel(x_ref, a_ref, b_ref, c_ref, w_ref, o_ref, wb_ref):
    @pl.when(pl.program_id(1) == 0)
    def _():
        wb_ref[...] = w_ref[...].astype(BF16)
    mix = jnp.concatenate([a_ref[...].astype(BF16), b_ref[...].astype(BF16), c_ref[...].astype(BF16)], axis=1)
    o_ref[...] = x_ref[...] + _dot(mix, wb_ref[...])


def _outproj(x, a, b, c, w_out, layer, tm, tn):
    m, d = x.shape
    k = w_out.shape[1]
    return pl.pallas_call(
        _outproj_kernel,
        grid=(d // tn, m // tm),
        in_specs=[pl.BlockSpec((tm, tn), lambda j, i: (i, j)),
                  pl.BlockSpec((tm, a.shape[1]), lambda j, i: (i, 0)),
                  pl.BlockSpec((tm, b.shape[1]), lambda j, i: (i, 0)),
                  pl.BlockSpec((tm, c.shape[1]), lambda j, i: (i, 0)),
                  pl.BlockSpec((None, k, tn), lambda j, i: (layer, 0, j))],
        out_specs=pl.BlockSpec((tm, tn), lambda j, i: (i, j)),
        out_shape=jax.ShapeDtypeStruct((m, d), F32),
        scratch_shapes=[pltpu.VMEM((k, tn), BF16)],
        compiler_params=_cparams(2),
        name="out_proj",
    )(x, a, b, c, w_out)


def _gating_kernel(u_ref, v_ref, z_ref, w_ref, b_ref, o_ref):
    rows = u_ref.shape[0]
    ri = lax.broadcasted_iota(jnp.int32, (CHUNK_MLP, CHUNK_MLP), 0)
    ci = lax.broadcasted_iota(jnp.int32, (CHUNK_MLP, CHUNK_MLP), 1)
    w = jnp.where(ri >= ci, w_ref[0], 0.0).astype(BF16)
    bias = b_ref[0]
    for c in range(rows // CHUNK_MLP):
        sl = slice(c * CHUNK_MLP, (c + 1) * CHUNK_MLP)
        vg = _gelu(v_ref[sl, :]).astype(BF16)
        mixed = _dot(w, vg) + bias
        o_ref[sl, :] = (_gelu(u_ref[sl, :]) * mixed * _silu(z_ref[sl, :])).astype(o_ref.dtype)


def _gating_prompt(proj, chunk_w, bias_b, rows):
    m = proj.shape[0]
    nb = HEAD_DIM
    return pl.pallas_call(
        _gating_kernel,
        grid=(A_GROUPS, m // rows),
        in_specs=[pl.BlockSpec((rows, nb), lambda g, i: (i, OFF_AU // nb + g)),
                  pl.BlockSpec((rows, nb), lambda g, i: (i, OFF_AV // nb + g)),
                  pl.BlockSpec((rows, nb), lambda g, i: (i, OFF_AZ // nb + g)),
                  pl.BlockSpec((1, CHUNK_MLP, CHUNK_MLP), lambda g, i: (g, 0, 0)),
                  pl.BlockSpec((1, CHUNK_MLP, nb), lambda g, i: (g, 0, 0))],
        out_specs=pl.BlockSpec((rows, nb), lambda g, i: (i, g)),
        out_shape=jax.ShapeDtypeStruct((m, A_W), BF16),
        compiler_params=_cparams(2),
        name="gating_prompt",
    )(proj, proj, proj, chunk_w, bias_b)


GDN_HEADS_PER_STEP = 4
GDN_ROWS_PER_STEP = 512


def _split2(a):
    hi = a.astype(BF16)
    lo = (a - hi.astype(F32)).astype(BF16)
    return hi, lo


def _dot3(ah, al, bh, bl):
    return _dot(ah, bh) + (_dot(ah, bl) + _dot(al, bh))


def _gdn_prompt_kernel(q_ref, k_ref, v_ref, z_ref, ab_ref, wq_ref, wk_ref, wv_ref,
                       alog_ref, dtb_ref, gain_ref, o_ref, s_ref, st_ref, tail_ref):
    hg = pl.program_id(1)
    tb = pl.program_id(2)
    rows = q_ref.shape[1]
    heads = GDN_HEADS_PER_STEP
    pair = 2 * GDN_CHUNK
    lane = lax.broadcasted_iota(jnp.int32, (1, HEAD_DIM), 1)
    alog = alog_ref[...]
    dtb_row = dtb_ref[...]

    ri = lax.broadcasted_iota(jnp.int32, (pair, pair), 0)
    ci = lax.broadcasted_iota(jnp.int32, (pair, pair), 1)
    same = (ri >= GDN_CHUNK) == (ci >= GDN_CHUNK)
    incl = same & (ri >= ci)
    strict = same & (ri > ci)
    incl16 = jnp.where(incl, 1.0, 0.0).astype(BF16)
    eye = jnp.where(ri == ci, 1.0, 0.0)
    row_lo = lax.broadcasted_iota(jnp.int32, (pair, HEAD_DIM), 0) < GDN_CHUNK
    gain = gain_ref[...]

    @pl.when(tb == 0)
    def _():
        st_ref[...] = jnp.zeros_like(st_ref)
        tail_ref[...] = jnp.zeros_like(tail_ref)

    hs = range(heads)
    cols = [slice(g * HEAD_DIM, (g + 1) * HEAD_DIM) for g in hs]
    sel_a = [lane == (hg * heads + g) for g in hs]
    sel_b = [lane == (B_HEADS + hg * heads + g) for g in hs]
    neg_a = [-jnp.exp(jnp.sum(jnp.where(sel_a[g], alog, 0.0), axis=1, keepdims=True)) for g in hs]
    dtb = [jnp.sum(jnp.where(sel_a[g], dtb_row, 0.0), axis=1, keepdims=True) for g in hs]

    def body(i, carry):
        r0 = pl.multiple_of(i * pair, pair)
        rp = pl.multiple_of(jnp.maximum(r0 - 8, 0), 8)
        ab = ab_ref[0, pl.ds(r0, pair), :]

        def conv(x_ref, w_ref, part, g):
            cur = x_ref[0, pl.ds(r0, pair), cols[g]]
            prev = jnp.where(i > 0, x_ref[0, pl.ds(rp, 8), cols[g]], tail_ref[part, :, cols[g]])
            win = jnp.concatenate([prev, cur], axis=0)
            w = w_ref[:, cols[g]]
            acc = win[5:5 + pair] * w[0:1]
            for j in range(1, CONV_WIDTH):
                acc = acc + win[5 + j:5 + j + pair] * w[j:j + 1]
            return _silu(acc)

        def l2n(x):
            return x * lax.rsqrt(jnp.sum(x * x, axis=-1, keepdims=True) + EPS)

        q = [l2n(conv(q_ref, wq_ref, 0, g)) * (HEAD_DIM ** -0.5) for g in hs]
        k = [l2n(conv(k_ref, wk_ref, 1, g)) for g in hs]
        v = [conv(v_ref, wv_ref, 2, g) for g in hs]
        g_col = [neg_a[g] * jax.nn.softplus(jnp.sum(jnp.where(sel_a[g], ab, 0.0), axis=1, keepdims=True) + dtb[g])
                 for g in hs]
        beta = [jax.nn.sigmoid(jnp.sum(jnp.where(sel_b[g], ab, 0.0), axis=1, keepdims=True)) for g in hs]

        def cumsum3(gc):
            g1 = jnp.broadcast_to(gc, (pair, HEAD_DIM))
            g1h = g1.astype(BF16)
            r1 = g1 - g1h.astype(F32)
            g1m = r1.astype(BF16)
            g1l = (r1 - g1m.astype(F32)).astype(BF16)
            return g1h, g1m, g1l

        gparts = [cumsum3(g_col[g]) for g in hs]
        gcb = [_dot(incl16, gp[0]) + (_dot(incl16, gp[1]) + _dot(incl16, gp[2])) for gp in gparts]
        egc = [jnp.exp(x) for x in gcb]
        glast = [jnp.where(row_lo, x[GDN_CHUNK - 1:GDN_CHUNK, :], x[pair - 1:pair, :]) for x in gcb]
        kdec = [k[g] * jnp.exp(glast[g] - gcb[g]) for g in hs]
        decay = [jnp.where(incl, jnp.exp(jnp.where(incl, x - x.T, 0.0)), 0.0) for x in gcb]

        kb = [k[g] * beta[g] for g in hs]
        k16 = [x.astype(BF16) for x in k]
        lmat = [jnp.where(strict, _dot_nt(kb[g].astype(BF16), k16[g]) * decay[g], 0.0) for g in hs]
        tinv = [eye - x for x in lmat]
        lsp = [_split2(x) for x in lmat]
        pw = [_dot3(lh, ll, lh, ll) for lh, ll in lsp]
        for step in range(5):
            psp = [_split2(x) for x in pw]
            tsp = [_split2(x) for x in tinv]
            tinv = [tinv[g] + _dot3(tsp[g][0], tsp[g][1], psp[g][0], psp[g][1]) for g in hs]
            if step < 4:
                pw = [_dot3(ph, pl_, ph, pl_) for ph, pl_ in psp]
        t16 = [x.astype(BF16) for x in tinv]
        u = [_dot(t16[g], (v[g] * beta[g]).astype(BF16)) for g in hs]
        w16 = [_dot(t16[g], (kb[g] * egc[g]).astype(BF16)).astype(BF16) for g in hs]
        a_intra = [(_dot_nt(q[g].astype(BF16), k16[g]) * decay[g]).astype(BF16) for g in hs]
        qdec = [(q[g] * egc[g]).astype(BF16) for g in hs]

        s = [st_ref[g] for g in hs]
        outs = [[] for _ in hs]
        for c in range(2):
            sl = slice(c * GDN_CHUNK, (c + 1) * GDN_CHUNK)
            in_chunk = row_lo if c == 0 else jnp.logical_not(row_lo)
            s16 = [x.astype(BF16) for x in s]
            vn16 = [jnp.where(in_chunk, u[g] - _dot(w16[g], s16[g]), 0.0).astype(BF16) for g in hs]
            for g in hs:
                outs[g].append(_dot(qdec[g][sl], s16[g]) + _dot(a_intra[g][sl], vn16[g]))
            gt = [jnp.exp(x[(c + 1) * GDN_CHUNK - 1:(c + 1) * GDN_CHUNK, :]) for x in gcb]
            kd = [jnp.where(in_chunk, x, 0.0).astype(BF16) for x in kdec]
            s = [s[g] * gt[g] + _dot_tn(kd[g], vn16[g]) for g in hs]
        for g in hs:
            st_ref[g] = s[g]
            o = jnp.concatenate(outs[g], axis=0)
            o = o * lax.rsqrt(jnp.mean(o * o, axis=-1, keepdims=True) + EPS) * gain
            o_ref[0, pl.ds(r0, pair), cols[g]] = (
                o * _silu(z_ref[0, pl.ds(r0, pair), cols[g]])).astype(o_ref.dtype)
        return carry

    lax.fori_loop(0, rows // pair, body, 0)
    for part, x_ref in enumerate((q_ref, k_ref, v_ref)):
        tail_ref[part] = x_ref[0, rows - 8:rows, :]

    @pl.when(tb == pl.num_programs(2) - 1)
    def _():
        s_ref[0] = st_ref[...]


def _gdn_prompt(proj3, ab3, conv_w, alog, dtb, gain):
    bn, t, _ = proj3.shape
    heads = GDN_HEADS_PER_STEP
    rows = min(GDN_ROWS_PER_STEP, t)
    width = heads * HEAD_DIM
    n_hg = B_HEADS // heads
    col = lambda off: (lambda b, hg, tb: (b, tb, off // width + hg))
    wcol = lambda part: (lambda b, hg, tb: (0, part * n_hg + hg))
    vec = pl.BlockSpec((1, HEAD_DIM), lambda b, hg, tb: (0, 0))
    return pl.pallas_call(
        _gdn_prompt_kernel,
        grid=(bn, n_hg, t // rows),
        in_specs=[pl.BlockSpec((1, rows, width), col(OFF_BQ)),
                  pl.BlockSpec((1, rows, width), col(OFF_BK)),
                  pl.BlockSpec((1, rows, width), col(OFF_BV)),
                  pl.BlockSpec((1, rows, width), col(OFF_BZ)),
                  pl.BlockSpec((1, rows, GATE_PAD), lambda b, hg, tb: (b, tb, 0)),
                  pl.BlockSpec((CONV_WIDTH, width), wcol(0)),
                  pl.BlockSpec((CONV_WIDTH, width), wcol(1)),
                  pl.BlockSpec((CONV_WIDTH, width), wcol(2)),
                  vec, vec, vec],
        out_specs=[pl.BlockSpec((1, rows, width), lambda b, hg, tb: (b, tb, hg)),
                   pl.BlockSpec((1, heads, HEAD_DIM, HEAD_DIM), lambda b, hg, tb: (b, hg, 0, 0))],
        out_shape=[jax.ShapeDtypeStruct((bn, t, B_W), BF16),
                   jax.ShapeDtypeStruct((bn, B_HEADS, HEAD_DIM, HEAD_DIM), F32)],
        scratch_shapes=[pltpu.VMEM((heads, HEAD_DIM, HEAD_DIM), F32),
                        pltpu.VMEM((3, 8, width), F32)],
        compiler_params=_cparams(3),
        name="gdn_prompt",
    )(proj3, proj3, proj3, proj3, ab3, conv_w, conv_w, conv_w, alog, dtb, gain)


AUG_LANE0 = QK_DIM
POS_SPLIT = 64
ATTN_BLOCK = 512


def _half_rmsnorm(x, gain2):
    lo = lax.broadcasted_iota(jnp.int32, x.shape, x.ndim - 1) < QK_DIM
    x2 = x * x
    s_lo = jnp.sum(jnp.where(lo, x2, 0.0), axis=-1, keepdims=True)
    s_hi = jnp.sum(jnp.where(lo, 0.0, x2), axis=-1, keepdims=True)
    ms = jnp.where(lo, s_lo, s_hi) * (1.0 / QK_DIM)
    return x * lax.rsqrt(ms + EPS) * gain2


def _alibi_tables(slopes_np, t):
    import ml_dtypes
    bf = ml_dtypes.bfloat16
    hi = slopes_np.astype(bf).astype(np.float32)
    mid = (slopes_np - hi).astype(bf).astype(np.float32)
    lo = (slopes_np - hi - mid).astype(bf).astype(np.float32)
    qaug = np.zeros((C_HEADS, HEAD_DIM), np.float32)
    for n, piece in enumerate((hi, mid, lo)):
        qaug[:, AUG_LANE0 + n] = POS_SPLIT * piece
        qaug[:, AUG_LANE0 + 3 + n] = piece
    pos = np.arange(t)
    kaug = np.zeros((t, HEAD_DIM), np.float32)
    kaug[:, AUG_LANE0:AUG_LANE0 + 3] = (pos // POS_SPLIT)[:, None]
    kaug[:, AUG_LANE0 + 3:AUG_LANE0 + 6] = (pos % POS_SPLIT)[:, None]
    return jnp.asarray(qaug.reshape(1, C_W)), jnp.asarray(kaug)


def _cprep_kernel(q_ref, k_ref, v_ref, gq_ref, gk_ref, qaug_ref, kaug_ref,
                  q1_ref, q2_ref, kn_ref, k1_ref, k2_ref, vo_ref, vb_ref):
    gq = gq_ref[...]
    gk = gk_ref[...]
    kaug = kaug_ref[...]
    lo = lax.broadcasted_iota(jnp.int32, kaug.shape, 1) < QK_DIM
    for h in range(C_HEADS):
        sl = slice(h * HEAD_DIM, (h + 1) * HEAD_DIM)
        qn = _half_rmsnorm(q_ref[0, :, sl], gq) * (QK_DIM ** -0.5)
        kn = _half_rmsnorm(k_ref[0, :, sl], gk)
        qaug = qaug_ref[:, sl]
        q1_ref[0, :, sl] = jnp.where(lo, qn, qaug).astype(BF16)
        q2_ref[0, :, sl] = jnp.where(lo, pltpu.roll(qn, QK_DIM, 1), qaug).astype(BF16)
        kn_ref[0, h] = kn
        k1_ref[0, :, sl] = jnp.where(lo, kn, kaug).astype(BF16)
        k2_ref[0, :, sl] = jnp.where(lo, pltpu.roll(kn, QK_DIM, 1), kaug).astype(BF16)
        vo_ref[0, h] = v_ref[0, :, sl]
    vb_ref[...] = v_ref[...].astype(BF16)


def _cprep_prompt(proj3, gq2, gk2, qaug, kaug, tr):
    bn, t, _ = proj3.shape
    blk = lambda off: pl.BlockSpec((1, tr, C_W), lambda b, r: (b, r, off // C_W))
    out = pl.BlockSpec((1, tr, C_W), lambda b, r: (b, r, 0))
    out_hm = pl.BlockSpec((1, C_HEADS, tr, HEAD_DIM), lambda b, r: (b, 0, r, 0))
    vec = pl.BlockSpec((1, HEAD_DIM), lambda b, r: (0, 0))
    sd = lambda dt: jax.ShapeDtypeStruct((bn, t, C_W), dt)
    sd_hm = jax.ShapeDtypeStruct((bn, C_HEADS, t, HEAD_DIM), F32)
    return pl.pallas_call(
        _cprep_kernel,
        grid=(bn, t // tr),
        in_specs=[blk(OFF_CQ), blk(OFF_CK), blk(OFF_CV), vec, vec,
                  pl.BlockSpec((1, C_W), lambda b, r: (0, 0)),
                  pl.BlockSpec((tr, HEAD_DIM), lambda b, r: (r, 0))],
        out_specs=[out, out, out_hm, out, out, out_hm, out],
        out_shape=[sd(BF16), sd(BF16), sd_hm, sd(BF16), sd(BF16), sd_hm, sd(BF16)],
        compiler_params=_cparams(2),
        name="cprep_prompt",
    )(proj3, proj3, proj3, gq2, gk2, qaug, kaug)


def _lambda_value(lamv, lam_init):
    e1 = jnp.exp(jnp.sum(lamv[0:1] * lamv[1:2], axis=-1, keepdims=True))
    e2 = jnp.exp(jnp.sum(lamv[2:3] * lamv[3:4], axis=-1, keepdims=True))
    return e1 - e2 + lam_init


def _attn_prompt_kernel(q1_ref, q2_ref, k1_ref, k2_ref, v_ref, z_ref, lamv_ref, gain_ref, o_ref,
                        *, lam_init, tq):
    qi = pl.program_id(2)
    q1 = q1_ref[0]
    q2 = q2_ref[0]
    causal = (lax.broadcasted_iota(jnp.int32, (tq, tq), 1) <= lax.broadcasted_iota(jnp.int32, (tq, tq), 0))

    def step(j, carry, masked):
        m, l, acc = carry[0::3], carry[1::3], carry[2::3]
        c0 = pl.multiple_of(j * tq, tq)
        vblk = v_ref[0, pl.ds(c0, tq), :]
        s = [_dot_nt(q1, k1_ref[0, pl.ds(c0, tq), :]), _dot_nt(q2, k2_ref[0, pl.ds(c0, tq), :])]
        if masked:
            s = [jnp.where(causal, x, NEG_BIG) for x in s]
        m_new = [jnp.maximum(m[c], jnp.max(s[c], axis=-1, keepdims=True)) for c in range(2)]
        alpha = [jnp.exp(m[c] - m_new[c]) for c in range(2)]
        p = [jnp.exp(s[c] - m_new[c]) for c in range(2)]
        l = [alpha[c] * l[c] + jnp.sum(p[c], axis=-1, keepdims=True) for c in range(2)]
        pv = [_dot(p[c].astype(BF16), vblk) for c in range(2)]
        acc = [alpha[c] * acc[c] + pv[c] for c in range(2)]
        return m_new[0], l[0], acc[0], m_new[1], l[1], acc[1]

    m0 = jnp.full((tq, 1), NEG_BIG, F32)
    l0 = jnp.zeros((tq, 1), F32)
    a0 = jnp.zeros((tq, HEAD_DIM), F32)
    carry = lax.fori_loop(0, qi, lambda j, c: step(j, c, False), (m0, l0, a0, m0, l0, a0))
    m1, l1, a1, m2, l2, a2 = step(qi, carry, True)
    lam = _lambda_value(lamv_ref[...], lam_init)
    o = a1 / l1 - lam * (a2 / l2)
    o = o * lax.rsqrt(jnp.mean(o * o, axis=-1, keepdims=True) + EPS) * gain_ref[...] * (1.0 - lam_init)
    o_ref[0] = (o * _silu(z_ref[0])).astype(o_ref.dtype)


def _attn_prompt(q1, q2, k1, k2, vb, proj3, lamv, gain, lam_init, tq):
    bn, t, _ = q1.shape
    nb = HEAD_DIM
    kern = functools.partial(_attn_prompt_kernel, lam_init=lam_init, tq=tq)
    qspec = pl.BlockSpec((1, tq, nb), lambda b, h, i: (b, i, h))
    kspec = pl.BlockSpec((1, t, nb), lambda b, h, i: (b, 0, h))
    return pl.pallas_call(
        kern,
        grid=(bn, C_HEADS, t // tq),
        in_specs=[qspec, qspec, kspec, kspec, kspec,
                  pl.BlockSpec((1, tq, nb), lambda b, h, i: (b, i, OFF_CZ // nb + h)),
                  pl.BlockSpec((4, QK_DIM), lambda b, h, i: (0, 0)),
                  pl.BlockSpec((1, nb), lambda b, h, i: (0, 0))],
        out_specs=pl.BlockSpec((1, tq, nb), lambda b, h, i: (b, i, h)),
        out_shape=jax.ShapeDtypeStruct((bn, t, C_W), BF16),
        compiler_params=_cparams(3),
        name="attn_prompt",
    )(q1, q2, k1, k2, vb, proj3, lamv, gain)


def _sample_mix_kernel(p_ref, ab_ref, cbuf_ref, cw_ref, st_ref, w00_ref, b0_ref, alog_ref, dtb_ref,
                       ggain_ref, gq_ref, gk_ref,
                       aout_ref, av_ref, bout_ref, snew_ref, qn_ref, kn_ref, vn_ref, zc_ref):
    nb = HEAD_DIM
    p = p_ref[0]
    av = _gelu(p[:, OFF_AV:OFF_AV + A_W])
    av_ref[0] = av
    mixed = w00_ref[...] * av + b0_ref[...]
    aout_ref[0] = _gelu(p[:, OFF_AU:OFF_AU + A_W]) * mixed * _silu(p[:, OFF_AZ:OFF_AZ + A_W])

    cb = cbuf_ref[0]
    cw = cw_ref[...]
    x = p[:, OFF_BQ:OFF_BQ + 3 * B_W]
    acc = cb[0:1] * cw[0:1] + cb[1:2] * cw[1:2] + cb[2:3] * cw[2:3] + x * cw[3:4]
    act = _silu(acc)
    ab = ab_ref[0]
    g_row = -jnp.exp(alog_ref[...]) * jax.nn.softplus(ab + dtb_ref[...])
    beta_row = jax.nn.sigmoid(ab)
    ggain = ggain_ref[...]
    row8 = lax.broadcasted_iota(jnp.int32, (8, nb), 0)
    for h in range(B_HEADS):
        q = act[:, h * nb:(h + 1) * nb]
        k = act[:, B_W + h * nb:B_W + (h + 1) * nb]
        v = act[:, 2 * B_W + h * nb:2 * B_W + (h + 1) * nb]
        q = q * lax.rsqrt(jnp.sum(q * q, axis=-1, keepdims=True) + EPS) * (nb ** -0.5)
        k = k * lax.rsqrt(jnp.sum(k * k, axis=-1, keepdims=True) + EPS)
        eg = jnp.exp(g_row[:, h:h + 1])
        beta = beta_row[:, B_HEADS + h:B_HEADS + h + 1]
        s = st_ref[0, h]
        lhs = jnp.where(row8 == 0, k * (beta * eg), jnp.where(row8 == 1, q * eg, 0.0))
        rs = _dot_hi(lhs, s)
        v_new = v * beta - rs[0:1]
        o = rs[1:2] + jnp.sum(q * k, axis=-1, keepdims=True) * v_new
        k8 = jnp.where(row8 == 0, k, 0.0)
        v8 = jnp.where(row8 == 0, v_new, 0.0)
        snew_ref[0, h] = s * eg + _dot_tn(k8, v8, precision=HIGHEST)
        o = o * lax.rsqrt(jnp.mean(o * o, axis=-1, keepdims=True) + EPS) * ggain
        bout_ref[0, :, h * nb:(h + 1) * nb] = o * _silu(p[:, OFF_BZ + h * nb:OFF_BZ + (h + 1) * nb])

    gq = gq_ref[...]
    gk = gk_ref[...]
    for h in range(C_HEADS):
        qn_ref[0, h] = _half_rmsnorm(p[:, OFF_CQ + h * nb:OFF_CQ + (h + 1) * nb], gq) * (QK_DIM ** -0.5)
        kn_ref[0, h] = _half_rmsnorm(p[:, OFF_CK + h * nb:OFF_CK + (h + 1) * nb], gk)
        vn_ref[0, h] = p[:, OFF_CV + h * nb:OFF_CV + (h + 1) * nb]
        zc_ref[0, h] = p[:, OFF_CZ + h * nb:OFF_CZ + (h + 1) * nb]


def _sample_mix(proj3, ab3, cbuf, conv_w, state, w00, b0, alog, dtb, ggain, gq2, gk2):
    bn = proj3.shape[0]
    nb = HEAD_DIM
    row = lambda w: pl.BlockSpec((1, 1, w), lambda b: (b, 0, 0))
    full2 = lambda a: pl.BlockSpec(a.shape, lambda b: (0, 0))
    st = pl.BlockSpec((1, B_HEADS, nb, nb), lambda b: (b, 0, 0, 0))
    heads = pl.BlockSpec((1, C_HEADS, 1, nb), lambda b: (b, 0, 0, 0))
    sd = lambda w: jax.ShapeDtypeStruct((bn, 1, w), F32)
    hd = jax.ShapeDtypeStruct((bn, C_HEADS, 1, nb), F32)
    return pl.pallas_call(
        _sample_mix_kernel,
        grid=(bn,),
        in_specs=[row(MAIN_COLS), row(GATE_PAD),
                  pl.BlockSpec((1, CONV_WIDTH - 1, 3 * B_W), lambda b: (b, 0, 0)),
                  full2(conv_w), st, full2(w00), full2(b0), full2(alog), full2(dtb),
                  full2(ggain), full2(gq2), full2(gk2)],
        out_specs=[row(A_W), row(A_W), row(B_W), st, heads, heads, heads, heads],
        out_shape=[sd(A_W), sd(A_W), sd(B_W),
                   jax.ShapeDtypeStruct(state.shape, F32), hd, hd, hd, hd],
        compiler_params=_cparams(1),
        name="sample_mix",
    )(proj3, ab3, cbuf, conv_w, state, w00, b0, alog, dtb, ggain, gq2, gk2)


def _paged_attn_kernel(pt_ref, qn_ref, kn_ref, vn_ref, z_ref, slope_ref, lamv_ref, gain_ref, *rest,
                       lam_init, past_len, n_groups):
    del pt_ref
    pp = PAGES_PER_STEP
    k_refs = rest[:pp]
    v_refs = rest[pp:2 * pp]
    o_ref = rest[2 * pp]
    sc_ref, mrun_ref, lrun_ref, acc_ref, q3_ref, snew_ref = rest[2 * pp + 1:]
    ph = pl.program_id(1)
    g = pl.program_id(2)
    page = k_refs[0].shape[1]
    sshape = mrun_ref.shape

    @pl.when((ph == 0) & (g == 0))
    def _():
        q = jnp.broadcast_to(qn_ref[0], q3_ref.shape)
        ri = lax.broadcasted_iota(jnp.int32, q3_ref.shape, 1)
        ci = lax.broadcasted_iota(jnp.int32, q3_ref.shape, 2)
        q3_ref[...] = jnp.where((ci // QK_DIM) == ri, q, 0.0)
        mrun_ref[...] = jnp.full(sshape, NEG_BIG, F32)
        lrun_ref[...] = jnp.zeros(sshape, F32)
        acc_ref[...] = jnp.zeros(acc_ref.shape, F32)

    @pl.when(ph == 0)
    def _():
        q3 = q3_ref[...].astype(BF16)
        slope = slope_ref[...]
        tok = lax.broadcasted_iota(jnp.int32, sshape, 2)
        mrun = mrun_ref[...]
        for i in range(pp):
            pg = g * pp + i
            s = _bdot_nt(q3, k_refs[i][...].astype(BF16))
            dist = (past_len - (pg * page + tok)).astype(F32)
            s = s - slope * dist
            sc_ref[pg] = s
            mrun = jnp.maximum(mrun, s)
        mrun_ref[...] = mrun

    @pl.when((ph == 1) & (g == 0))
    def _():
        s_new = jnp.sum(q3_ref[...].astype(BF16).astype(F32) * kn_ref[0], axis=-1, keepdims=True)
        m = jnp.maximum(jnp.max(mrun_ref[...], axis=-1, keepdims=True), s_new)
        mrun_ref[...] = jnp.broadcast_to(m, sshape)
        snew_ref[...] = jnp.broadcast_to(s_new, sshape)

    @pl.when(ph == 1)
    def _():
        m = mrun_ref[...]
        lrun = lrun_ref[...]
        acc = acc_ref[...]
        for i in range(pp):
            pg = g * pp + i
            p = jnp.exp(sc_ref[pg] - m)
            lrun = lrun + p
            acc = acc + _bdot(p.astype(BF16), v_refs[i][...].astype(BF16))
        lrun_ref[...] = lrun
        acc_ref[...] = acc

    @pl.when((ph == 1) & (g == n_groups - 1))
    def _():
        m = mrun_ref[:, :, 0:1]
        p_new = jnp.exp(snew_ref[:, :, 0:1] - m)
        l = jnp.sum(lrun_ref[...], axis=-1, keepdims=True) + p_new
        normed = (acc_ref[...] + p_new * vn_ref[0]) / l
        lam = _lambda_value(lamv_ref[...], lam_init)
        o = normed[:, 0:1, :] - lam * normed[:, 1:2, :]
        o = o * lax.rsqrt(jnp.mean(o * o, axis=-1, keepdims=True) + EPS) * gain_ref[...] * (1.0 - lam_init)
        o_ref[0] = o * _silu(z_ref[0])


def _paged_attn(page_table, qn, kn, vn, zc, cache_k, cache_v, layer, slope3, lamv, gain, lam_init):
    bn, n_pages = page_table.shape
    page = cache_k.shape[3]
    pp = PAGES_PER_STEP
    assert n_pages % pp == 0
    n_groups = n_pages // pp
    past_len = n_pages * page
    head_row = pl.BlockSpec((1, C_HEADS, 1, HEAD_DIM), lambda b, ph, g, pt: (b, 0, 0, 0))

    def k_spec(i):
        def imap(b, ph, g, pt):
            grp = jnp.where(ph == 0, g, n_groups - 1)
            return (layer, pt[b, grp * pp + i], 0, 0, 0)
        return pl.BlockSpec((None, None, C_HEADS, page, HEAD_DIM), imap)

    def v_spec(i):
        def imap(b, ph, g, pt):
            grp = jnp.where(ph == 0, 0, g)
            return (layer, pt[b, grp * pp + i], 0, 0, 0)
        return pl.BlockSpec((None, None, C_HEADS, page, HEAD_DIM), imap)

    def full(a):
        nd = a.ndim
        return pl.BlockSpec(a.shape, lambda b, ph, g, pt: (0,) * nd)

    sshape = (C_HEADS, SCORE_ROWS, page)
    qshape = (C_HEADS, SCORE_ROWS, HEAD_DIM)
    kern = functools.partial(_paged_attn_kernel, lam_init=lam_init, past_len=past_len, n_groups=n_groups)
    grid_spec = pltpu.PrefetchScalarGridSpec(
        num_scalar_prefetch=1,
        grid=(bn, 2, n_groups),
        in_specs=[head_row, head_row, head_row, head_row, full(slope3), full(lamv), full(gain)]
                 + [k_spec(i) for i in range(pp)] + [v_spec(i) for i in range(pp)],
        out_specs=head_row,
        scratch_shapes=[pltpu.VMEM((n_pages,) + sshape, F32),
                        pltpu.VMEM(sshape, F32),
                        pltpu.VMEM(sshape, F32),
                        pltpu.VMEM(qshape, F32),
                        pltpu.VMEM(qshape, F32),
                        pltpu.VMEM(sshape, F32)])
    return pl.pallas_call(
        kern,
        grid_spec=grid_spec,
        out_shape=jax.ShapeDtypeStruct((bn, C_HEADS, 1, HEAD_DIM), F32),
        compiler_params=_cparams(3),
        name="paged_attn",
    )(page_table, qn, kn, vn, zc, slope3, lamv, gain, *([cache_k] * pp), *([cache_v] * pp))


def _pad_lanes(v, width=GATE_PAD):
    return jnp.pad(v.astype(F32), (0, width - v.shape[0])).reshape(1, width)


def _layer_params(l, norm_gain, chunk_w, chunk_b, gdn_conv_w, gdn_a_log, gdn_dt_bias,
                  gdn_norm_gain, attn_q_norm, attn_k_norm, lq1, lk1, lq2, lk2, subln):
    return dict(
        layer=l, norm_gain=norm_gain[l], chunk_w=chunk_w[l],
        bias_b=jnp.broadcast_to(chunk_b[l][:, :, None], (A_GROUPS, CHUNK_MLP, HEAD_DIM)),
        w00=jnp.repeat(chunk_w[l][:, 0, 0], HEAD_DIM).reshape(1, A_W),
        b0=jnp.repeat(chunk_b[l][:, 0], HEAD_DIM).reshape(1, A_W),
        conv_w=gdn_conv_w[l], alog=_pad_lanes(gdn_a_log[l]), dtb=_pad_lanes(gdn_dt_bias[l]),
        ggain=gdn_norm_gain[l].reshape(1, HEAD_DIM),
        gq2=jnp.tile(attn_q_norm[l], 2).reshape(1, HEAD_DIM),
        gk2=jnp.tile(attn_k_norm[l], 2).reshape(1, HEAD_DIM),
        lamv=jnp.stack([lq1[l], lk1[l], lq2[l], lk2[l]]).astype(F32),
        subln=subln[l].reshape(1, HEAD_DIM),
        lam_init=0.8 - 0.6 * math.exp(-0.3 * l),
    )


MAIN_TN = 512


def _prompt_layer(x, lp, w_in_t, w_out, qaug, kaug):
    bn, t, d = x.shape
    m = bn * t
    x2 = x.reshape(m, d)
    tm = min(512, m)
    layer = lp['layer']
    h = _rmsnorm(x2, lp['norm_gain'], min(256, m))
    proj = _inproj(h, w_in_t, layer, functools.partial(_main_row_start, tn=MAIN_TN), MAIN_COLS, tm, MAIN_TN)
    gates = _inproj(h, w_in_t, layer, lambda j: GATE_SRC, GATE_PAD, tm, GATE_PAD)
    proj3 = proj.reshape(bn, t, MAIN_COLS)
    gates3 = gates.reshape(bn, t, GATE_PAD)

    a_out = _gating_prompt(proj, lp['chunk_w'], lp['bias_b'], min(512, t))
    b_out, s_fin = _gdn_prompt(proj3, gates3, lp['conv_w'], lp['alog'], lp['dtb'], lp['ggain'])
    q1, q2, kn, k1, k2, vo, vb = _cprep_prompt(proj3, lp['gq2'], lp['gk2'], qaug, kaug, min(256, t))
    c_out = _attn_prompt(q1, q2, k1, k2, vb, proj3, lp['lamv'], lp['subln'], lp['lam_init'],
                         min(ATTN_BLOCK, t))

    y = _outproj(x2, a_out, b_out.reshape(m, B_W), c_out.reshape(m, C_W), w_out, layer, tm, MAIN_TN)
    conv_new = proj3[:, t - (CONV_WIDTH - 1):, OFF_BQ:OFF_BQ + 3 * B_W]
    return y.reshape(bn, t, d), kn, vo, s_fin, conv_new


def _sample_layer(x, lp, w_in_t, w_out, slope3, state, cbuf, cache_k, cache_v, page_table):
    bn, t, d = x.shape
    x2 = x.reshape(bn, d)
    layer = lp['layer']
    proj = _norm_inproj_small(x2, lp['norm_gain'], w_in_t, layer,
                              functools.partial(_main_row_start, tn=MAIN_TN), MAIN_COLS, MAIN_TN)
    gates = _norm_inproj_small(x2, lp['norm_gain'], w_in_t, layer, lambda j: GATE_SRC, GATE_PAD, GATE_PAD)
    proj3 = proj.reshape(bn, 1, MAIN_COLS)
    gates3 = gates.reshape(bn, 1, GATE_PAD)
    a_out, a_v, b_out, s_new, qn, kn, vn, zc = _sample_mix(
        proj3, gates3, cbuf, lp['conv_w'], state, lp['w00'], lp['b0'], lp['alog'], lp['dtb'],
        lp['ggain'], lp['gq2'], lp['gk2'])
    c_out = _paged_attn(page_table, qn, kn, vn, zc, cache_k, cache_v, layer, slope3,
                        lp['lamv'], lp['subln'], lp['lam_init'])
    y = _outproj(x2, a_out.reshape(bn, A_W), b_out.reshape(bn, B_W), c_out.reshape(bn, C_W),
                 w_out, layer, bn, MAIN_TN)
    conv_new = jnp.concatenate([cbuf[:, 1:], proj3[:, :, OFF_BQ:OFF_BQ + 3 * B_W]], axis=1)
    return (y.reshape(bn, t, d), kn.reshape(bn, 1, C_HEADS, HEAD_DIM), vn.reshape(bn, 1, C_HEADS, HEAD_DIM),
            s_new, conv_new, a_v)


def kernel(x_prompt, x_sample, cache_attn_k, cache_attn_v, state_gdn, state_gdn_conv, page_table,
           norm_gain, w_in, w_out, chunk_w, chunk_b, gdn_conv_w, gdn_a_log, gdn_dt_bias, gdn_norm_gain,
           attn_q_norm, attn_k_norm, lambda_q1, lambda_k1, lambda_q2, lambda_k2, attn_subln_gain):
    depth = w_in.shape[0]
    slopes_np = _alibi_slopes(C_HEADS)
    page = cache_attn_k.shape[2]
    slope3 = jnp.asarray(np.broadcast_to(slopes_np[:, None, None], (C_HEADS, SCORE_ROWS, page)).copy())
    qaug, kaug = _alibi_tables(slopes_np, x_prompt.shape[1])
    w_in_t = jnp.transpose(w_in, (0, 2, 1))
    ck = jnp.transpose(cache_attn_k, (0, 1, 3, 2, 4))
    cv = jnp.transpose(cache_attn_v, (0, 1, 3, 2, 4))

    yp, ys = x_prompt, x_sample
    outs = [[] for _ in range(9)]
    for l in range(depth):
        lp = _layer_params(l, norm_gain, chunk_w, chunk_b, gdn_conv_w, gdn_a_log, gdn_dt_bias,
                           gdn_norm_gain, attn_q_norm, attn_k_norm, lambda_q1, lambda_k1, lambda_q2,
                           lambda_k2, attn_subln_gain)
        yp, pk, pv, ps, pc = _prompt_layer(yp, lp, w_in_t, w_out, qaug, kaug)
        ys, sk, sv, ss, sc, sa = _sample_layer(ys, lp, w_in_t, w_out, slope3, state_gdn[l], state_gdn_conv[l],
                                               ck, cv, page_table)
        for lst, val in zip(outs, (pk, pv, ps, pc, sk, sv, ss, sc, sa)):
            lst.append(val)
    res = [jnp.stack(o) for o in outs]
    res[0] = jnp.transpose(res[0], (0, 1, 3, 2, 4))
    res[1] = jnp.transpose(res[1], (0, 1, 3, 2, 4))
    return (yp, ys) + tuple(res)
```

```python
import functools
import math

import jax
import jax.numpy as jnp
import numpy as np
from jax import lax
from jax.experimental import pallas as pl
from jax.experimental.pallas import tpu as pltpu

F32 = jnp.float32
BF16 = jnp.bfloat16
HIGHEST = lax.Precision.HIGHEST

HEAD_DIM = 128
A_GROUPS = 8
B_HEADS = 12
C_HEADS = 12
A_W = A_GROUPS * HEAD_DIM
B_W = B_HEADS * HEAD_DIM
C_W = C_HEADS * HEAD_DIM
QK_DIM = HEAD_DIM // 2
CHUNK_MLP = 128
GDN_CHUNK = 64
CONV_WIDTH = 4
EPS = 1e-6
INV_SQRT2 = 0.7071067811865476
NEG_BIG = -1e30

OFF_AU, OFF_AV, OFF_AZ = 0, A_W, 2 * A_W
OFF_BQ = 3 * A_W
OFF_BK = OFF_BQ + B_W
OFF_BV = OFF_BK + B_W
OFF_BZ = OFF_BV + B_W
OFF_CQ = OFF_BZ + B_W
OFF_CK = OFF_CQ + C_W
OFF_CV = OFF_CK + C_W
OFF_CZ = OFF_CV + C_W
MAIN_COLS = OFF_CZ + C_W
GATE_SRC = 3 * A_W + 4 * B_W
GATE_PAD = 128

VMEM_LIMIT = 56 * 1024 * 1024
PAGES_PER_STEP = 8
SCORE_ROWS = 8


def _cparams(n_axes):
    return pltpu.CompilerParams(dimension_semantics=("arbitrary",) * n_axes,
                                vmem_limit_bytes=VMEM_LIMIT)


def _gelu(x):
    return 0.5 * x * (1.0 + lax.erf(x * INV_SQRT2))


def _silu(x):
    return x * jax.nn.sigmoid(x)


def _dot(a, b):
    return jnp.dot(a, b, preferred_element_type=F32)


def _dot_hi(a, b):
    return jnp.dot(a, b, preferred_element_type=F32, precision=HIGHEST)


def _dot_nt(a, b):
    return lax.dot_general(a, b, (((1,), (1,)), ((), ())), preferred_element_type=F32)


def _dot_tn(a, b, precision=None):
    return lax.dot_general(a, b, (((0,), (0,)), ((), ())), preferred_element_type=F32,
                           precision=precision)


def _bdot_nt(a, b):
    return lax.dot_general(a, b, (((2,), (2,)), ((0,), (0,))), preferred_element_type=F32)


def _bdot(a, b):
    return lax.dot_general(a, b, (((2,), (1,)), ((0,), (0,))), preferred_element_type=F32)


def _alibi_slopes(n):
    def pow2(m):
        start = 2.0 ** (-8.0 / m)
        return [start ** (i + 1) for i in range(m)]
    p = 2 ** int(math.floor(math.log2(n)))
    s = pow2(p)
    if p < n:
        s = s + pow2(2 * p)[0::2][: n - p]
    return np.array(s, dtype=np.float32)


def _rmsnorm_kernel(x_ref, g_ref, o_ref):
    x = x_ref[...]
    ms = jnp.mean(x * x, axis=-1, keepdims=True)
    o_ref[...] = (x * lax.rsqrt(ms + EPS) * g_ref[...]).astype(o_ref.dtype)


def _rmsnorm(x, gain, tm):
    m, d = x.shape
    return pl.pallas_call(
        _rmsnorm_kernel,
        grid=(m // tm,),
        in_specs=[pl.BlockSpec((tm, d), lambda i: (i, 0)),
                  pl.BlockSpec((1, d), lambda i: (0, 0))],
        out_specs=pl.BlockSpec((tm, d), lambda i: (i, 0)),
        out_shape=jax.ShapeDtypeStruct((m, d), BF16),
        compiler_params=_cparams(1),
        name="rmsnorm",
    )(x, gain.reshape(1, d))


def _main_row_start(j, tn):
    return pl.multiple_of(j * tn + jnp.where(j * tn >= GATE_SRC, 2 * B_HEADS, 0), 8)


def _wt_spec(layer, tn, k, row_start, n_grid_axes):
    if n_grid_axes == 2:
        imap = lambda j, i: (layer, row_start(j), 0)
    else:
        imap = lambda j: (layer, row_start(j), 0)
    return pl.BlockSpec((pl.Element(1), pl.Element(tn), pl.Element(k)), imap)


def _inproj_kernel(a_ref, w_ref, o_ref, wt_ref):
    @pl.when(pl.program_id(1) == 0)
    def _():
        wt_ref[...] = w_ref[0].T.astype(BF16)
    o_ref[...] = _dot(a_ref[...], wt_ref[...])


def _inproj(a, w_t, layer, row_start, n_out, tm, tn):
    m, k = a.shape
    return pl.pallas_call(
        _inproj_kernel,
        grid=(n_out // tn, m // tm),
        in_specs=[pl.BlockSpec((tm, k), lambda j, i: (i, 0)),
                  _wt_spec(layer, tn, k, row_start, 2)],
        out_specs=pl.BlockSpec((tm, tn), lambda j, i: (i, j)),
        out_shape=jax.ShapeDtypeStruct((m, n_out), F32),
        scratch_shapes=[pltpu.VMEM((k, tn), BF16)],
        compiler_params=_cparams(2),
        name="in_proj",
    )(a, w_t)


def _norm_inproj_kernel(x_ref, g_ref, w_ref, o_ref):
    x = x_ref[...]
    ms = jnp.mean(x * x, axis=-1, keepdims=True)
    h = (x * lax.rsqrt(ms + EPS) * g_ref[...]).astype(BF16)
    o_ref[...] = _dot_nt(h, w_ref[0].astype(BF16))


def _norm_inproj_small(x, gain, w_t, layer, row_start, n_out, tn):
    m, k = x.shape
    return pl.pallas_call(
        _norm_inproj_kernel,
        grid=(n_out // tn,),
        in_specs=[pl.BlockSpec((m, k), lambda j: (0, 0)),
                  pl.BlockSpec((1, k), lambda j: (0, 0)),
                  _wt_spec(layer, tn, k, row_start, 1)],
        out_specs=pl.BlockSpec((m, tn), lambda j: (0, j)),
        out_shape=jax.ShapeDtypeStruct((m, n_out), F32),
        compiler_params=_cparams(1),
        name="in_proj_sample",
    )(x, gain.reshape(1, k), w_t)


def _outproj_kernel(x_ref, a_ref, b_ref, c_ref, w_ref, o_ref, wb_ref):
    @pl.when(pl.program_id(1) == 0)
    def _():
        wb_ref[...] = w_ref[...].astype(BF16)
    ka = a_ref.shape[1]
    kb = ka + b_ref.shape[1]
    acc = _dot(a_ref[...].astype(BF16), wb_ref[0:ka, :])
    acc = acc + _dot(b_ref[...].astype(BF16), wb_ref[ka:kb, :])
    acc = acc + _dot(c_ref[...].astype(BF16), wb_ref[kb:, :])
    o_ref[...] = x_ref[...] + acc


def _outproj(x, a, b, c, w_out, layer, tm, tn):
    m, d = x.shape
    k = w_out.shape[1]
    return pl.pallas_call(
        _outproj_kernel,
        grid=(d // tn, m // tm),
        in_specs=[pl.BlockSpec((tm, tn), lambda j, i: (i, j)),
                  pl.BlockSpec((tm, a.shape[1]), lambda j, i: (i, 0)),
                  pl.BlockSpec((tm, b.shape[1]), lambda j, i: (i, 0)),
                  pl.BlockSpec((tm, c.shape[1]), lambda j, i: (i, 0)),
                  pl.BlockSpec((None, k, tn), lambda j, i: (layer, 0, j))],
        out_specs=pl.BlockSpec((tm, tn), lambda j, i: (i, j)),
        out_shape=jax.ShapeDtypeStruct((m, d), F32),
        scratch_shapes=[pltpu.VMEM((k, tn), BF16)],
        compiler_params=_cparams(2),
        name="out_proj",
    )(x, a, b, c, w_out)


def _gating_kernel(u_ref, v_ref, z_ref, w_ref, b_ref, o_ref):
    rows = u_ref.shape[0]
    ri = lax.broadcasted_iota(jnp.int32, (CHUNK_MLP, CHUNK_MLP), 0)
    ci = lax.broadcasted_iota(jnp.int32, (CHUNK_MLP, CHUNK_MLP), 1)
    w = jnp.where(ri >= ci, w_ref[0], 0.0).astype(BF16)
    bias = b_ref[0]
    for c in range(rows // CHUNK_MLP):
        sl = slice(c * CHUNK_MLP, (c + 1) * CHUNK_MLP)
        vg = _gelu(v_ref[sl, :]).astype(BF16)
        mixed = _dot(w, vg) + bias
        o_ref[sl, :] = (_gelu(u_ref[sl, :]) * mixed * _silu(z_ref[sl, :])).astype(o_ref.dtype)


def _gating_prompt(proj, chunk_w, bias_b, rows):
    m = proj.shape[0]
    nb = HEAD_DIM
    return pl.pallas_call(
        _gating_kernel,
        grid=(A_GROUPS, m // rows),
        in_specs=[pl.BlockSpec((rows, nb), lambda g, i: (i, OFF_AU // nb + g)),
                  pl.BlockSpec((rows, nb), lambda g, i: (i, OFF_AV // nb + g)),
                  pl.BlockSpec((rows, nb), lambda g, i: (i, OFF_AZ // nb + g)),
                  pl.BlockSpec((1, CHUNK_MLP, CHUNK_MLP), lambda g, i: (g, 0, 0)),
                  pl.BlockSpec((1, CHUNK_MLP, nb), lambda g, i: (g, 0, 0))],
        out_specs=pl.BlockSpec((rows, nb), lambda g, i: (i, g)),
        out_shape=jax.ShapeDtypeStruct((m, A_W), BF16),
        compiler_params=_cparams(2),
        name="gating_prompt",
    )(proj, proj, proj, chunk_w, bias_b)


GDN_HEADS_PER_STEP = 12
GDN_ROWS_PER_STEP = 512
NEUMANN_SPLIT_STEPS = 2


def _split2(a):
    hi = a.astype(BF16)
    lo = (a - hi.astype(F32)).astype(BF16)
    return hi, lo


def _dot3(ah, al, bh, bl):
    return _dot(ah, bh) + (_dot(ah, bl) + _dot(al, bh))


def _gdn_prompt_kernel(q_ref, k_ref, v_ref, z_ref, ab_ref, wq_ref, wk_ref, wv_ref,
                       alog_ref, dtb_ref, gain_ref, o_ref, s_ref, st_ref, tail_ref):
    hg = pl.program_id(1)
    tb = pl.program_id(2)
    rows = q_ref.shape[1]
    heads = GDN_HEADS_PER_STEP
    pair = 2 * GDN_CHUNK

    ri = lax.broadcasted_iota(jnp.int32, (pair, pair), 0)
    ci = lax.broadcasted_iota(jnp.int32, (pair, pair), 1)
    same = (ri >= GDN_CHUNK) == (ci >= GDN_CHUNK)
    incl = same & (ri >= ci)
    strict = same & (ri > ci)
    incl16 = jnp.where(incl, 1.0, 0.0).astype(BF16)
    eye = jnp.where(ri == ci, 1.0, 0.0)
    row_lo = lax.broadcasted_iota(jnp.int32, (pair, HEAD_DIM), 0) < GDN_CHUNK
    gain = gain_ref[...]

    @pl.when(tb == 0)
    def _():
        st_ref[...] = jnp.zeros_like(st_ref)
        tail_ref[...] = jnp.zeros_like(tail_ref)

    hs = range(heads)
    cols = [slice(g * HEAD_DIM, (g + 1) * HEAD_DIM) for g in hs]
    pick_a = [jnp.where(ri == (hg * heads + g), 1.0, 0.0).astype(BF16) for g in hs]
    pick_b = [jnp.where(ri == (B_HEADS + hg * heads + g), 1.0, 0.0).astype(BF16) for g in hs]
    neg_a_row = -jnp.exp(alog_ref[...])
    dtb_row = dtb_ref[...]

    def split3(x):
        hi = x.astype(BF16)
        r = x - hi.astype(F32)
        mid = r.astype(BF16)
        return hi, mid, (r - mid.astype(F32)).astype(BF16)

    def body(i, carry):
        r0 = pl.multiple_of(i * pair, pair)
        rp = pl.multiple_of(jnp.maximum(r0 - 8, 0), 8)
        ab = ab_ref[0, pl.ds(r0, pair), :]

        def conv(x_ref, w_ref, part, g):
            cur = x_ref[0, pl.ds(r0, pair), cols[g]]
            prev = jnp.where(i > 0, x_ref[0, pl.ds(rp, 8), cols[g]], tail_ref[part, :, cols[g]])
            win = jnp.concatenate([prev, cur], axis=0)
            w = w_ref[:, cols[g]]
            acc = win[5:5 + pair] * w[0:1]
            for j in range(1, CONV_WIDTH):
                acc = acc + win[5 + j:5 + j + pair] * w[j:j + 1]
            return _silu(acc)

        def l2n(x):
            return x * lax.rsqrt(jnp.sum(x * x, axis=-1, keepdims=True) + EPS)

        q = [l2n(conv(q_ref, wq_ref, 0, g)) * (HEAD_DIM ** -0.5) for g in hs]
        k = [l2n(conv(k_ref, wk_ref, 1, g)) for g in hs]
        v = [conv(v_ref, wv_ref, 2, g) for g in hs]
        g_all = neg_a_row * jax.nn.softplus(ab + dtb_row)
        gp = split3(g_all)
        gc_all = _dot(incl16, gp[0]) + (_dot(incl16, gp[1]) + _dot(incl16, gp[2]))
        cp = split3(gc_all)
        bp = _split2(jax.nn.sigmoid(ab))
        gcb = [_dot(cp[0], pick_a[g]) + (_dot(cp[1], pick_a[g]) + _dot(cp[2], pick_a[g])) for g in hs]
        beta = [_dot(bp[0], pick_b[g]) + _dot(bp[1], pick_b[g]) for g in hs]
        egc = [jnp.exp(x) for x in gcb]
        glast = [jnp.where(row_lo, x[GDN_CHUNK - 1:GDN_CHUNK, :], x[pair - 1:pair, :]) for x in gcb]
        kdec = [k[g] * jnp.exp(glast[g] - gcb[g]) for g in hs]
        decay = [jnp.where(incl, jnp.exp(jnp.where(incl, x - x.T, 0.0)), 0.0) for x in gcb]

        kb = [k[g] * beta[g] for g in hs]
        k16 = [x.astype(BF16) for x in k]
        lmat = [jnp.where(strict, _dot_nt(kb[g].astype(BF16), k16[g]) * decay[g], 0.0) for g in hs]
        tinv = [eye - x for x in lmat]
        lsp = [_split2(x) for x in lmat]
        pw = [_dot3(lh, ll, lh, ll) for lh, ll in lsp]
        for step in range(5):
            if step < NEUMANN_SPLIT_STEPS:
                psp = [_split2(x) for x in pw]
                tsp = [_split2(x) for x in tinv]
                tinv = [tinv[g] + _dot3(tsp[g][0], tsp[g][1], psp[g][0], psp[g][1]) for g in hs]
                pw = [_dot3(ph, pl_, ph, pl_) for ph, pl_ in psp]
            else:
                p16 = [x.astype(BF16) for x in pw]
                tinv = [tinv[g] + _dot(tinv[g].astype(BF16), p16[g]) for g in hs]
                if step < 4:
                    pw = [_dot(x, x) for x in p16]
        t16 = [x.astype(BF16) for x in tinv]
        u = [_dot(t16[g], (v[g] * beta[g]).astype(BF16)) for g in hs]
        w16 = [_dot(t16[g], (kb[g] * egc[g]).astype(BF16)).astype(BF16) for g in hs]
        a_intra = [(_dot_nt(q[g].astype(BF16), k16[g]) * decay[g]).astype(BF16) for g in hs]
        qdec = [(q[g] * egc[g]).astype(BF16) for g in hs]

        s = [st_ref[g] for g in hs]
        outs = [[] for _ in hs]
        for c in range(2):
            sl = slice(c * GDN_CHUNK, (c + 1) * GDN_CHUNK)
            in_chunk = row_lo if c == 0 else jnp.logical_not(row_lo)
            s16 = [x.astype(BF16) for x in s]
            vn16 = [jnp.where(in_chunk, u[g] - _dot(w16[g], s16[g]), 0.0).astype(BF16) for g in hs]
            for g in hs:
                outs[g].append(_dot(qdec[g][sl], s16[g]) + _dot(a_intra[g][sl], vn16[g]))
            gt = [jnp.exp(x[(c + 1) * GDN_CHUNK - 1:(c + 1) * GDN_CHUNK, :]) for x in gcb]
            kd = [jnp.where(in_chunk, x, 0.0).astype(BF16) for x in kdec]
            s = [s[g] * gt[g] + _dot_tn(kd[g], vn16[g]) for g in hs]
        for g in hs:
            st_ref[g] = s[g]
            o = jnp.concatenate(outs[g], axis=0)
            o = o * lax.rsqrt(jnp.mean(o * o, axis=-1, keepdims=True) + EPS) * gain
            o_ref[0, pl.ds(r0, pair), cols[g]] = (
                o * _silu(z_ref[0, pl.ds(r0, pair), cols[g]])).astype(o_ref.dtype)
        return carry

    lax.fori_loop(0, rows // pair, body, 0)
    for part, x_ref in enumerate((q_ref, k_ref, v_ref)):
        tail_ref[part] = x_ref[0, rows - 8:rows, :]

    @pl.when(tb == pl.num_programs(2) - 1)
    def _():
        s_ref[0] = st_ref[...]


def _gdn_prompt(proj3, ab3, conv_w, alog, dtb, gain):
    bn, t, _ = proj3.shape
    heads = GDN_HEADS_PER_STEP
    rows = min(GDN_ROWS_PER_STEP, t)
    width = heads * HEAD_DIM
    n_hg = B_HEADS // heads
    col = lambda off: (lambda b, hg, tb: (b, tb, off // width + hg))
    wcol = lambda part: (lambda b, hg, tb: (0, part * n_hg + hg))
    vec = pl.BlockSpec((1, HEAD_DIM), lambda b, hg, tb: (0, 0))
    return pl.pallas_call(
        _gdn_prompt_kernel,
        grid=(bn, n_hg, t // rows),
        in_specs=[pl.BlockSpec((1, rows, width), col(OFF_BQ)),
                  pl.BlockSpec((1, rows, width), col(OFF_BK)),
                  pl.BlockSpec((1, rows, width), col(OFF_BV)),
                  pl.BlockSpec((1, rows, width), col(OFF_BZ)),
                  pl.BlockSpec((1, rows, GATE_PAD), lambda b, hg, tb: (b, tb, 0)),
                  pl.BlockSpec((CONV_WIDTH, width), wcol(0)),
                  pl.BlockSpec((CONV_WIDTH, width), wcol(1)),
                  pl.BlockSpec((CONV_WIDTH, width), wcol(2)),
                  vec, vec, vec],
        out_specs=[pl.BlockSpec((1, rows, width), lambda b, hg, tb: (b, tb, hg)),
                   pl.BlockSpec((1, heads, HEAD_DIM, HEAD_DIM), lambda b, hg, tb: (b, hg, 0, 0))],
        out_shape=[jax.ShapeDtypeStruct((bn, t, B_W), BF16),
                   jax.ShapeDtypeStruct((bn, B_HEADS, HEAD_DIM, HEAD_DIM), F32)],
        scratch_shapes=[pltpu.VMEM((heads, HEAD_DIM, HEAD_DIM), F32),
                        pltpu.VMEM((3, 8, width), F32)],
        compiler_params=_cparams(3),
        name="gdn_prompt",
    )(proj3, proj3, proj3, proj3, ab3, conv_w, conv_w, conv_w, alog, dtb, gain)


AUG_LANE0 = QK_DIM
POS_SPLIT = 64
ATTN_BLOCK = 512


def _half_rmsnorm(x, gain2):
    lo = lax.broadcasted_iota(jnp.int32, x.shape, x.ndim - 1) < QK_DIM
    x2 = x * x
    s_lo = jnp.sum(jnp.where(lo, x2, 0.0), axis=-1, keepdims=True)
    s_hi = jnp.sum(jnp.where(lo, 0.0, x2), axis=-1, keepdims=True)
    ms = jnp.where(lo, s_lo, s_hi) * (1.0 / QK_DIM)
    return x * lax.rsqrt(ms + EPS) * gain2


def _alibi_tables(slopes_np, t):
    import ml_dtypes
    bf = ml_dtypes.bfloat16
    hi = slopes_np.astype(bf).astype(np.float32)
    mid = (slopes_np - hi).astype(bf).astype(np.float32)
    lo = (slopes_np - hi - mid).astype(bf).astype(np.float32)
    qaug = np.zeros((C_HEADS, HEAD_DIM), np.float32)
    for n, piece in enumerate((hi, mid, lo)):
        qaug[:, AUG_LANE0 + n] = POS_SPLIT * piece
        qaug[:, AUG_LANE0 + 3 + n] = piece
    pos = np.arange(t)
    kaug = np.zeros((t, HEAD_DIM), np.float32)
    kaug[:, AUG_LANE0:AUG_LANE0 + 3] = (pos // POS_SPLIT)[:, None]
    kaug[:, AUG_LANE0 + 3:AUG_LANE0 + 6] = (pos % POS_SPLIT)[:, None]
    return jnp.asarray(qaug.reshape(1, C_W)), jnp.asarray(kaug)


def _cprep_kernel(q_ref, k_ref, v_ref, gq_ref, gk_ref, qaug_ref, kaug_ref, *rest):
    q1_ref, q2_ref, kn_ref, k1_ref, k2_ref, vo_ref, vb_ref = rest[-7:]
    gq = gq_ref[...]
    gk = gk_ref[...]
    kaug = kaug_ref[...]
    lo = lax.broadcasted_iota(jnp.int32, kaug.shape, 1) < QK_DIM
    for h in range(C_HEADS):
        sl = slice(h * HEAD_DIM, (h + 1) * HEAD_DIM)
        qn = _half_rmsnorm(q_ref[0, :, sl], gq) * (QK_DIM ** -0.5)
        kn = _half_rmsnorm(k_ref[0, :, sl], gk)
        qaug = qaug_ref[:, sl]
        q1_ref[0, :, sl] = jnp.where(lo, qn, qaug).astype(BF16)
        q2_ref[0, :, sl] = jnp.where(lo, pltpu.roll(qn, QK_DIM, 1), qaug).astype(BF16)
        kn_ref[0, h] = kn
        k1_ref[0, :, sl] = jnp.where(lo, kn, kaug).astype(BF16)
        k2_ref[0, :, sl] = jnp.where(lo, pltpu.roll(kn, QK_DIM, 1), kaug).astype(BF16)
        vo_ref[0, h] = v_ref[0, :, sl]
    vb_ref[...] = v_ref[...].astype(BF16)


def _cprep_prompt(proj3, gq2, gk2, qaug, kaug, tr, layer, depth, kv_all):
    bn, t, _ = proj3.shape
    blk = lambda off: pl.BlockSpec((1, tr, C_W), lambda b, r: (b, r, off // C_W))
    out = pl.BlockSpec((1, tr, C_W), lambda b, r: (b, r, 0))
    out_hm = pl.BlockSpec((None, 1, C_HEADS, tr, HEAD_DIM), lambda b, r: (layer, b, 0, r, 0))
    vec = pl.BlockSpec((1, HEAD_DIM), lambda b, r: (0, 0))
    sd = lambda dt: jax.ShapeDtypeStruct((bn, t, C_W), dt)
    sd_hm = jax.ShapeDtypeStruct((depth, bn, C_HEADS, t, HEAD_DIM), F32)
    in_specs = [blk(OFF_CQ), blk(OFF_CK), blk(OFF_CV), vec, vec,
                pl.BlockSpec((1, C_W), lambda b, r: (0, 0)),
                pl.BlockSpec((tr, HEAD_DIM), lambda b, r: (r, 0))]
    args = [proj3, proj3, proj3, gq2, gk2, qaug, kaug]
    aliases = {}
    if kv_all is not None:
        in_specs += [pl.BlockSpec(memory_space=pl.ANY)] * 2
        aliases = {len(args): 2, len(args) + 1: 5}
        args += list(kv_all)
    return pl.pallas_call(
        _cprep_kernel,
        grid=(bn, t // tr),
        in_specs=in_specs,
        out_specs=[out, out, out_hm, out, out, out_hm, out],
        out_shape=[sd(BF16), sd(BF16), sd_hm, sd(BF16), sd(BF16), sd_hm, sd(BF16)],
        input_output_aliases=aliases,
        compiler_params=_cparams(2),
        name="cprep_prompt",
    )(*args)


def _lambda_value(lamv, lam_init):
    e1 = jnp.exp(jnp.sum(lamv[0:1] * lamv[1:2], axis=-1, keepdims=True))
    e2 = jnp.exp(jnp.sum(lamv[2:3] * lamv[3:4], axis=-1, keepdims=True))
    return e1 - e2 + lam_init


def _attn_prompt_kernel(q1_ref, q2_ref, k1_ref, k2_ref, v_ref, z_ref, lamv_ref, gain_ref, o_ref,
                        *, lam_init, tq):
    qi = pl.program_id(2)
    q1 = q1_ref[0]
    q2 = q2_ref[0]
    causal = (lax.broadcasted_iota(jnp.int32, (tq, tq), 1) <= lax.broadcasted_iota(jnp.int32, (tq, tq), 0))

    def step(j, carry, masked):
        m, l, acc = carry[0::3], carry[1::3], carry[2::3]
        c0 = pl.multiple_of(j * tq, tq)
        vblk = v_ref[0, pl.ds(c0, tq), :]
        s = [_dot_nt(q1, k1_ref[0, pl.ds(c0, tq), :]), _dot_nt(q2, k2_ref[0, pl.ds(c0, tq), :])]
        if masked:
            s = [jnp.where(causal, x, NEG_BIG) for x in s]
        m_new = [jnp.maximum(m[c], jnp.max(s[c], axis=-1, keepdims=True)) for c in range(2)]
        alpha = [jnp.exp(m[c] - m_new[c]) for c in range(2)]
        p = [jnp.exp(s[c] - m_new[c]) for c in range(2)]
        l = [alpha[c] * l[c] + jnp.sum(p[c], axis=-1, keepdims=True) for c in range(2)]
        pv = [_dot(p[c].astype(BF16), vblk) for c in range(2)]
        acc = [alpha[c] * acc[c] + pv[c] for c in range(2)]
        return m_new[0], l[0], acc[0], m_new[1], l[1], acc[1]

    m0 = jnp.full((tq, 1), NEG_BIG, F32)
    l0 = jnp.zeros((tq, 1), F32)
    a0 = jnp.zeros((tq, HEAD_DIM), F32)
    carry = lax.fori_loop(0, qi, lambda j, c: step(j, c, False), (m0, l0, a0, m0, l0, a0))
    m1, l1, a1, m2, l2, a2 = step(qi, carry, True)
    lam = _lambda_value(lamv_ref[...], lam_init)
    o = a1 / l1 - lam * (a2 / l2)
    o = o * lax.rsqrt(jnp.mean(o * o, axis=-1, keepdims=True) + EPS) * gain_ref[...] * (1.0 - lam_init)
    o_ref[0] = (o * _silu(z_ref[0])).astype(o_ref.dtype)


def _attn_prompt(q1, q2, k1, k2, vb, proj3, lamv, gain, lam_init, tq):
    bn, t, _ = q1.shape
    nb = HEAD_DIM
    kern = functools.partial(_attn_prompt_kernel, lam_init=lam_init, tq=tq)
    qspec = pl.BlockSpec((1, tq, nb), lambda b, h, i: (b, i, h))
    kspec = pl.BlockSpec((1, t, nb), lambda b, h, i: (b, 0, h))
    return pl.pallas_call(
        kern,
        grid=(bn, C_HEADS, t // tq),
        in_specs=[qspec, qspec, kspec, kspec, kspec,
                  pl.BlockSpec((1, tq, nb), lambda b, h, i: (b, i, OFF_CZ // nb + h)),
                  pl.BlockSpec((4, QK_DIM), lambda b, h, i: (0, 0)),
                  pl.BlockSpec((1, nb), lambda b, h, i: (0, 0))],
        out_specs=pl.BlockSpec((1, tq, nb), lambda b, h, i: (b, i, h)),
        out_shape=jax.ShapeDtypeStruct((bn, t, C_W), BF16),
        compiler_params=_cparams(3),
        name="attn_prompt",
    )(q1, q2, k1, k2, vb, proj3, lamv, gain)


def _sample_mix_kernel(p_ref, ab_ref, cbuf_ref, cw_ref, st_ref, w00_ref, b0_ref, alog_ref, dtb_ref,
                       ggain_ref, gq_ref, gk_ref,
                       aout_ref, av_ref, bout_ref, snew_ref, qn_ref, kn_ref, vn_ref, zc_ref):
    nb = HEAD_DIM
    p = p_ref[0]
    av = _gelu(p[:, OFF_AV:OFF_AV + A_W])
    av_ref[0] = av
    mixed = w00_ref[...] * av + b0_ref[...]
    aout_ref[0] = _gelu(p[:, OFF_AU:OFF_AU + A_W]) * mixed * _silu(p[:, OFF_AZ:OFF_AZ + A_W])

    cb = cbuf_ref[0]
    cw = cw_ref[...]
    x = p[:, OFF_BQ:OFF_BQ + 3 * B_W]
    acc = cb[0:1] * cw[0:1] + cb[1:2] * cw[1:2] + cb[2:3] * cw[2:3] + x * cw[3:4]
    act = _silu(acc)
    ab = ab_ref[0]
    g_row = -jnp.exp(alog_ref[...]) * jax.nn.softplus(ab + dtb_ref[...])
    beta_row = jax.nn.sigmoid(ab)
    ggain = ggain_ref[...]
    row8 = lax.broadcasted_iota(jnp.int32, (8, nb), 0)
    for h in range(B_HEADS):
        q = act[:, h * nb:(h + 1) * nb]
        k = act[:, B_W + h * nb:B_W + (h + 1) * nb]
        v = act[:, 2 * B_W + h * nb:2 * B_W + (h + 1) * nb]
        q = q * lax.rsqrt(jnp.sum(q * q, axis=-1, keepdims=True) + EPS) * (nb ** -0.5)
        k = k * lax.rsqrt(jnp.sum(k * k, axis=-1, keepdims=True) + EPS)
        eg = jnp.exp(g_row[:, h:h + 1])
        beta = beta_row[:, B_HEADS + h:B_HEADS + h + 1]
        s = st_ref[0, h]
        lhs = jnp.where(row8 == 0, k * (beta * eg), jnp.where(row8 == 1, q * eg, 0.0))
        rs = _dot_hi(lhs, s)
        v_new = v * beta - rs[0:1]
        o = rs[1:2] + jnp.sum(q * k, axis=-1, keepdims=True) * v_new
        k8 = jnp.where(row8 == 0, k, 0.0)
        v8 = jnp.where(row8 == 0, v_new, 0.0)
        snew_ref[0, h] = s * eg + _dot_tn(k8, v8, precision=HIGHEST)
        o = o * lax.rsqrt(jnp.mean(o * o, axis=-1, keepdims=True) + EPS) * ggain
        bout_ref[0, :, h * nb:(h + 1) * nb] = o * _silu(p[:, OFF_BZ + h * nb:OFF_BZ + (h + 1) * nb])

    gq = gq_ref[...]
    gk = gk_ref[...]
    for h in range(C_HEADS):
        qn_ref[0, h] = _half_rmsnorm(p[:, OFF_CQ + h * nb:OFF_CQ + (h + 1) * nb], gq) * (QK_DIM ** -0.5)
        kn_ref[0, h] = _half_rmsnorm(p[:, OFF_CK + h * nb:OFF_CK + (h + 1) * nb], gk)
        vn_ref[0, h] = p[:, OFF_CV + h * nb:OFF_CV + (h + 1) * nb]
        zc_ref[0, h] = p[:, OFF_CZ + h * nb:OFF_CZ + (h + 1) * nb]


def _sample_mix(proj3, ab3, cbuf, conv_w, state, w00, b0, alog, dtb, ggain, gq2, gk2):
    bn = proj3.shape[0]
    nb = HEAD_DIM
    row = lambda w: pl.BlockSpec((1, 1, w), lambda b: (b, 0, 0))
    full2 = lambda a: pl.BlockSpec(a.shape, lambda b: (0, 0))
    st = pl.BlockSpec((1, B_HEADS, nb, nb), lambda b: (b, 0, 0, 0))
    heads = pl.BlockSpec((1, C_HEADS, 1, nb), lambda b: (b, 0, 0, 0))
    sd = lambda w: jax.ShapeDtypeStruct((bn, 1, w), F32)
    hd = jax.ShapeDtypeStruct((bn, C_HEADS, 1, nb), F32)
    return pl.pallas_call(
        _sample_mix_kernel,
        grid=(bn,),
        in_specs=[row(MAIN_COLS), row(GATE_PAD),
                  pl.BlockSpec((1, CONV_WIDTH - 1, 3 * B_W), lambda b: (b, 0, 0)),
                  full2(conv_w), st, full2(w00), full2(b0), full2(alog), full2(dtb),
                  full2(ggain), full2(gq2), full2(gk2)],
        out_specs=[row(A_W), row(A_W), row(B_W), st, heads, heads, heads, heads],
        out_shape=[sd(A_W), sd(A_W), sd(B_W),
                   jax.ShapeDtypeStruct(state.shape, F32), hd, hd, hd, hd],
        compiler_params=_cparams(1),
        name="sample_mix",
    )(proj3, ab3, cbuf, conv_w, state, w00, b0, alog, dtb, ggain, gq2, gk2)


def _paged_attn_kernel(pt_ref, qn_ref, kn_ref, vn_ref, z_ref, slope_ref, lamv_ref, gain_ref, *rest,
                       lam_init, past_len, n_groups):
    del pt_ref
    n_scratch = 6
    pp = (len(rest) - 1 - n_scratch) // 2
    k_refs = rest[:pp]
    v_refs = rest[pp:2 * pp]
    o_ref = rest[2 * pp]
    sc_ref, mrun_ref, lrun_ref, acc_ref, q3_ref, snew_ref = rest[2 * pp + 1:]
    ph = pl.program_id(1)
    g = pl.program_id(2)
    page = k_refs[0].shape[1]
    sshape = mrun_ref.shape

    @pl.when((ph == 0) & (g == 0))
    def _():
        q = jnp.broadcast_to(qn_ref[0], q3_ref.shape)
        ri = lax.broadcasted_iota(jnp.int32, q3_ref.shape, 1)
        ci = lax.broadcasted_iota(jnp.int32, q3_ref.shape, 2)
        q3_ref[...] = jnp.where((ci // QK_DIM) == ri, q, 0.0)
        mrun_ref[...] = jnp.full(sshape, NEG_BIG, F32)
        lrun_ref[...] = jnp.zeros(sshape, F32)
        acc_ref[...] = jnp.zeros(acc_ref.shape, F32)

    @pl.when(ph == 0)
    def _():
        q3 = q3_ref[...].astype(BF16)
        slope = slope_ref[...]
        tok = lax.broadcasted_iota(jnp.int32, sshape, 2)
        mrun = mrun_ref[...]
        for i in range(pp):
            pg = g * pp + i
            s = _bdot_nt(q3, k_refs[i][...].astype(BF16))
            dist = (past_len - (pg * page + tok)).astype(F32)
            s = s - slope * dist
            sc_ref[pg] = s
            mrun = jnp.maximum(mrun, s)
        mrun_ref[...] = mrun

    @pl.when((ph == 1) & (g == 0))
    def _():
        s_new = jnp.sum(q3_ref[...].astype(BF16).astype(F32) * kn_ref[0], axis=-1, keepdims=True)
        m = jnp.maximum(jnp.max(mrun_ref[...], axis=-1, keepdims=True), s_new)
        mrun_ref[...] = jnp.broadcast_to(m, sshape)
        snew_ref[...] = jnp.broadcast_to(s_new, sshape)

    @pl.when(ph == 1)
    def _():
        m = mrun_ref[...]
        lrun = lrun_ref[...]
        acc = acc_ref[...]
        for i in range(pp):
            pg = g * pp + i
            p = jnp.exp(sc_ref[pg] - m)
            lrun = lrun + p
            acc = acc + _bdot(p.astype(BF16), v_refs[i][...].astype(BF16))
        lrun_ref[...] = lrun
        acc_ref[...] = acc

    @pl.when((ph == 1) & (g == n_groups - 1))
    def _():
        m = mrun_ref[:, :, 0:1]
        p_new = jnp.exp(snew_ref[:, :, 0:1] - m)
        l = jnp.sum(lrun_ref[...], axis=-1, keepdims=True) + p_new
        normed = (acc_ref[...] + p_new * vn_ref[0]) / l
        lam = _lambda_value(lamv_ref[...], lam_init)
        o = normed[:, 0:1, :] - lam * normed[:, 1:2, :]
        o = o * lax.rsqrt(jnp.mean(o * o, axis=-1, keepdims=True) + EPS) * gain_ref[...] * (1.0 - lam_init)
        o_ref[0] = o * _silu(z_ref[0])


def _paged_attn(page_table, qn, kn, vn, zc, cache_k, cache_v, layer, slope3, lamv, gain, lam_init):
    bn, n_pages = page_table.shape
    page = cache_k.shape[3]
    pp = math.gcd(PAGES_PER_STEP, n_pages)
    n_groups = n_pages // pp
    past_len = n_pages * page
    head_row = pl.BlockSpec((1, C_HEADS, 1, HEAD_DIM), lambda b, ph, g, pt: (b, 0, 0, 0))

    def k_spec(i):
        def imap(b, ph, g, pt):
            grp = jnp.where(ph == 0, g, n_groups - 1)
            return (layer, pt[b, grp * pp + i], 0, 0, 0)
        return pl.BlockSpec((None, None, C_HEADS, page, HEAD_DIM), imap)

    def v_spec(i):
        def imap(b, ph, g, pt):
            grp = jnp.where(ph == 0, 0, g)
            return (layer, pt[b, grp * pp + i], 0, 0, 0)
        return pl.BlockSpec((None, None, C_HEADS, page, HEAD_DIM), imap)

    def full(a):
        nd = a.ndim
        return pl.BlockSpec(a.shape, lambda b, ph, g, pt: (0,) * nd)

    sshape = (C_HEADS, SCORE_ROWS, page)
    qshape = (C_HEADS, SCORE_ROWS, HEAD_DIM)
    kern = functools.partial(_paged_attn_kernel, lam_init=lam_init, past_len=past_len, n_groups=n_groups)
    grid_spec = pltpu.PrefetchScalarGridSpec(
        num_scalar_prefetch=1,
        grid=(bn, 2, n_groups),
        in_specs=[head_row, head_row, head_row, head_row, full(slope3), full(lamv), full(gain)]
                 + [k_spec(i) for i in range(pp)] + [v_spec(i) for i in range(pp)],
        out_specs=head_row,
        scratch_shapes=[pltpu.VMEM((n_pages,) + sshape, F32),
                        pltpu.VMEM(sshape, F32),
                        pltpu.VMEM(sshape, F32),
                        pltpu.VMEM(qshape, F32),
                        pltpu.VMEM(qshape, F32),
                        pltpu.VMEM(sshape, F32)])
    return pl.pallas_call(
        kern,
        grid_spec=grid_spec,
        out_shape=jax.ShapeDtypeStruct((bn, C_HEADS, 1, HEAD_DIM), F32),
        compiler_params=_cparams(3),
        name="paged_attn",
    )(page_table, qn, kn, vn, zc, slope3, lamv, gain, *([cache_k] * pp), *([cache_v] * pp))


def _pad_lanes(v, width=GATE_PAD):
    return jnp.pad(v.astype(F32), (0, width - v.shape[0])).reshape(1, width)


def _layer_params(l, norm_gain, chunk_w, chunk_b, gdn_conv_w, gdn_a_log, gdn_dt_bias,
                  gdn_norm_gain, attn_q_norm, attn_k_norm, lq1, lk1, lq2, lk2, subln):
    return dict(
        layer=l, norm_gain=norm_gain[l], chunk_w=chunk_w[l],
        bias_b=jnp.broadcast_to(chunk_b[l][:, :, None], (A_GROUPS, CHUNK_MLP, HEAD_DIM)),
        w00=jnp.repeat(chunk_w[l][:, 0, 0], HEAD_DIM).reshape(1, A_W),
        b0=jnp.repeat(chunk_b[l][:, 0], HEAD_DIM).reshape(1, A_W),
        conv_w=gdn_conv_w[l], alog=_pad_lanes(gdn_a_log[l]), dtb=_pad_lanes(gdn_dt_bias[l]),
        ggain=gdn_norm_gain[l].reshape(1, HEAD_DIM),
        gq2=jnp.tile(attn_q_norm[l], 2).reshape(1, HEAD_DIM),
        gk2=jnp.tile(attn_k_norm[l], 2).reshape(1, HEAD_DIM),
        lamv=jnp.stack([lq1[l], lk1[l], lq2[l], lk2[l]]).astype(F32),
        subln=subln[l].reshape(1, HEAD_DIM),
        lam_init=0.8 - 0.6 * math.exp(-0.3 * l),
    )


MAIN_TN = 512
MAIN_TM = 1024


def _prompt_layer(x, lp, w_in_t, w_out, qaug, kaug, depth, kv_all):
    bn, t, d = x.shape
    m = bn * t
    x2 = x.reshape(m, d)
    tm = min(MAIN_TM, m)
    layer = lp['layer']
    h = _rmsnorm(x2, lp['norm_gain'], min(256, m))
    proj = _inproj(h, w_in_t, layer, functools.partial(_main_row_start, tn=MAIN_TN), MAIN_COLS, tm, MAIN_TN)
    gates = _inproj(h, w_in_t, layer, lambda j: GATE_SRC, GATE_PAD, tm, GATE_PAD)
    proj3 = proj.reshape(bn, t, MAIN_COLS)
    gates3 = gates.reshape(bn, t, GATE_PAD)

    a_out = _gating_prompt(proj, lp['chunk_w'], lp['bias_b'], min(512, t))
    b_out, s_fin = _gdn_prompt(proj3, gates3, lp['conv_w'], lp['alog'], lp['dtb'], lp['ggain'])
    q1, q2, kn, k1, k2, vo, vb = _cprep_prompt(proj3, lp['gq2'], lp['gk2'], qaug, kaug, min(256, t),
                                               layer, depth, kv_all)
    c_out = _attn_prompt(q1, q2, k1, k2, vb, proj3, lp['lamv'], lp['subln'], lp['lam_init'],
                         min(ATTN_BLOCK, t))

    y = _outproj(x2, a_out, b_out.reshape(m, B_W), c_out.reshape(m, C_W), w_out, layer, tm, MAIN_TN)
    conv_new = proj3[:, t - (CONV_WIDTH - 1):, OFF_BQ:OFF_BQ + 3 * B_W]
    return y.reshape(bn, t, d), (kn, vo), s_fin, conv_new


def _sample_layer(x, lp, w_in_t, w_out, slope3, state, cbuf, cache_k, cache_v, page_table):
    bn, t, d = x.shape
    x2 = x.reshape(bn, d)
    layer = lp['layer']
    proj = _norm_inproj_small(x2, lp['norm_gain'], w_in_t, layer,
                              functools.partial(_main_row_start, tn=MAIN_TN), MAIN_COLS, MAIN_TN)
    gates = _norm_inproj_small(x2, lp['norm_gain'], w_in_t, layer, lambda j: GATE_SRC, GATE_PAD, GATE_PAD)
    proj3 = proj.reshape(bn, 1, MAIN_COLS)
    gates3 = gates.reshape(bn, 1, GATE_PAD)
    a_out, a_v, b_out, s_new, qn, kn, vn, zc = _sample_mix(
        proj3, gates3, cbuf, lp['conv_w'], state, lp['w00'], lp['b0'], lp['alog'], lp['dtb'],
        lp['ggain'], lp['gq2'], lp['gk2'])
    c_out = _paged_attn(page_table, qn, kn, vn, zc, cache_k, cache_v, layer, slope3,
                        lp['lamv'], lp['subln'], lp['lam_init'])
    y = _outproj(x2, a_out.reshape(bn, A_W), b_out.reshape(bn, B_W), c_out.reshape(bn, C_W),
                 w_out, layer, bn, MAIN_TN)
    conv_new = jnp.concatenate([cbuf[:, 1:], proj3[:, :, OFF_BQ:OFF_BQ + 3 * B_W]], axis=1)
    return (y.reshape(bn, t, d), kn.reshape(bn, 1, C_HEADS, HEAD_DIM), vn.reshape(bn, 1, C_HEADS, HEAD_DIM),
            s_new, conv_new, a_v)


def kernel(x_prompt, x_sample, cache_attn_k, cache_attn_v, state_gdn, state_gdn_conv, page_table,
           norm_gain, w_in, w_out, chunk_w, chunk_b, gdn_conv_w, gdn_a_log, gdn_dt_bias, gdn_norm_gain,
           attn_q_norm, attn_k_norm, lambda_q1, lambda_k1, lambda_q2, lambda_k2, attn_subln_gain):
    depth = w_in.shape[0]
    slopes_np = _alibi_slopes(C_HEADS)
    page = cache_attn_k.shape[2]
    slope3 = jnp.asarray(np.broadcast_to(slopes_np[:, None, None], (C_HEADS, SCORE_ROWS, page)).copy())
    qaug, kaug = _alibi_tables(slopes_np, x_prompt.shape[1])
    w_in_t = jnp.transpose(w_in, (0, 2, 1))
    ck = jnp.transpose(cache_attn_k, (0, 1, 3, 2, 4))
    cv = jnp.transpose(cache_attn_v, (0, 1, 3, 2, 4))

    yp, ys = x_prompt, x_sample
    outs = [[] for _ in range(7)]
    kv_all = None
    for l in range(depth):
        lp = _layer_params(l, norm_gain, chunk_w, chunk_b, gdn_conv_w, gdn_a_log, gdn_dt_bias,
                           gdn_norm_gain, attn_q_norm, attn_k_norm, lambda_q1, lambda_k1, lambda_q2,
                           lambda_k2, attn_subln_gain)
        yp, kv_all, ps, pc = _prompt_layer(yp, lp, w_in_t, w_out, qaug, kaug, depth, kv_all)
        ys, sk, sv, ss, sc, sa = _sample_layer(ys, lp, w_in_t, w_out, slope3, state_gdn[l], state_gdn_conv[l],
                                               ck, cv, page_table)
        for lst, val in zip(outs, (ps, pc, sk, sv, ss, sc, sa)):
            lst.append(val)
    res = [jnp.stack(o) for o in outs]
    pk = jnp.transpose(kv_all[0], (0, 1, 3, 2, 4))
    pv = jnp.transpose(kv_all[1], (0, 1, 3, 2, 4))
    return (yp, ys, pk, pv) + tuple(res)
```

```python
import functools
import math

import jax
import jax.numpy as jnp
import numpy as np
from jax import lax
from jax.experimental import pallas as pl
from jax.experimental.pallas import tpu as pltpu

F32 = jnp.float32
BF16 = jnp.bfloat16
HIGHEST = lax.Precision.HIGHEST

HEAD_DIM = 128
A_GROUPS = 8
B_HEADS = 12
C_HEADS = 12
A_W = A_GROUPS * HEAD_DIM
B_W = B_HEADS * HEAD_DIM
C_W = C_HEADS * HEAD_DIM
QK_DIM = HEAD_DIM // 2
CHUNK_MLP = 128
GDN_CHUNK = 64
CONV_WIDTH = 4
EPS = 1e-6
INV_SQRT2 = 0.7071067811865476
NEG_BIG = -1e30

OFF_AU, OFF_AV, OFF_AZ = 0, A_W, 2 * A_W
OFF_BQ = 3 * A_W
OFF_BK = OFF_BQ + B_W
OFF_BV = OFF_BK + B_W
OFF_BZ = OFF_BV + B_W
OFF_CQ = OFF_BZ + B_W
OFF_CK = OFF_CQ + C_W
OFF_CV = OFF_CK + C_W
OFF_CZ = OFF_CV + C_W
MAIN_COLS = OFF_CZ + C_W
GATE_SRC = 3 * A_W + 4 * B_W
GATE_PAD = 128

VMEM_LIMIT = 56 * 1024 * 1024
PAGES_PER_STEP = 8
SCORE_ROWS = 8


def _cparams(n_axes):
    return pltpu.CompilerParams(dimension_semantics=("arbitrary",) * n_axes,
                                vmem_limit_bytes=VMEM_LIMIT)


def _gelu(x):
    return 0.5 * x * (1.0 + lax.erf(x * INV_SQRT2))


def _silu(x):
    return x * jax.nn.sigmoid(x)


def _dot(a, b):
    return jnp.dot(a, b, preferred_element_type=F32)


def _dot_hi(a, b):
    return jnp.dot(a, b, preferred_element_type=F32, precision=HIGHEST)


def _dot_nt(a, b):
    return lax.dot_general(a, b, (((1,), (1,)), ((), ())), preferred_element_type=F32)


def _dot_tn(a, b, precision=None):
    return lax.dot_general(a, b, (((0,), (0,)), ((), ())), preferred_element_type=F32,
                           precision=precision)


def _bdot_nt(a, b):
    return lax.dot_general(a, b, (((2,), (2,)), ((0,), (0,))), preferred_element_type=F32)


def _bdot(a, b):
    return lax.dot_general(a, b, (((2,), (1,)), ((0,), (0,))), preferred_element_type=F32)


def _alibi_slopes(n):
    def pow2(m):
        start = 2.0 ** (-8.0 / m)
        return [start ** (i + 1) for i in range(m)]
    p = 2 ** int(math.floor(math.log2(n)))
    s = pow2(p)
    if p < n:
        s = s + pow2(2 * p)[0::2][: n - p]
    return np.array(s, dtype=np.float32)


def _rmsnorm_kernel(x_ref, g_ref, o_ref):
    x = x_ref[...]
    ms = jnp.mean(x * x, axis=-1, keepdims=True)
    o_ref[...] = (x * lax.rsqrt(ms + EPS) * g_ref[...]).astype(o_ref.dtype)


def _rmsnorm(x, gain, tm):
    m, d = x.shape
    return pl.pallas_call(
        _rmsnorm_kernel,
        grid=(m // tm,),
        in_specs=[pl.BlockSpec((tm, d), lambda i: (i, 0)),
                  pl.BlockSpec((1, d), lambda i: (0, 0))],
        out_specs=pl.BlockSpec((tm, d), lambda i: (i, 0)),
        out_shape=jax.ShapeDtypeStruct((m, d), BF16),
        compiler_params=_cparams(1),
        name="rmsnorm",
    )(x, gain.reshape(1, d))


def _main_row_start(j, tn):
    return pl.multiple_of(j * tn + jnp.where(j * tn >= GATE_SRC, 2 * B_HEADS, 0), 8)


def _wt_spec(layer, tn, k, row_start, n_grid_axes):
    if n_grid_axes == 2:
        imap = lambda j, i: (layer, row_start(j), 0)
    else:
        imap = lambda j: (layer, row_start(j), 0)
    return pl.BlockSpec((pl.Element(1), pl.Element(tn), pl.Element(k)), imap)


def _inproj_kernel(a_ref, w_ref, o_ref, wt_ref):
    @pl.when(pl.program_id(1) == 0)
    def _():
        wt_ref[...] = w_ref[0].T.astype(BF16)
    o_ref[...] = _dot(a_ref[...], wt_ref[...])


def _inproj(a, w_t, layer, row_start, n_out, tm, tn):
    m, k = a.shape
    return pl.pallas_call(
        _inproj_kernel,
        grid=(n_out // tn, m // tm),
        in_specs=[pl.BlockSpec((tm, k), lambda j, i: (i, 0)),
                  _wt_spec(layer, tn, k, row_start, 2)],
        out_specs=pl.BlockSpec((tm, tn), lambda j, i: (i, j)),
        out_shape=jax.ShapeDtypeStruct((m, n_out), F32),
        scratch_shapes=[pltpu.VMEM((k, tn), BF16)],
        compiler_params=_cparams(2),
        name="in_proj",
    )(a, w_t)


def _norm_inproj_kernel(x_ref, g_ref, w_ref, o_ref):
    x = x_ref[...]
    ms = jnp.mean(x * x, axis=-1, keepdims=True)
    h = (x * lax.rsqrt(ms + EPS) * g_ref[...]).astype(BF16)
    o_ref[...] = _dot_nt(h, w_ref[0].astype(BF16))


def _norm_inproj_small(x, gain, w_t, layer, row_start, n_out, tn):
    m, k = x.shape
    return pl.pallas_call(
        _norm_inproj_kernel,
        grid=(n_out // tn,),
        in_specs=[pl.BlockSpec((m, k), lambda j: (0, 0)),
                  pl.BlockSpec((1, k), lambda j: (0, 0)),
                  _wt_spec(layer, tn, k, row_start, 1)],
        out_specs=pl.BlockSpec((m, tn), lambda j: (0, j)),
        out_shape=jax.ShapeDtypeStruct((m, n_out), F32),
        compiler_params=_cparams(1),
        name="in_proj_sample",
    )(x, gain.reshape(1, k), w_t)


def _outproj_kernel(x_ref, a_ref, b_ref, c_ref, w_ref, o_ref, wb_ref):
    @pl.when(pl.program_id(1) == 0)
    def _():
        wb_ref[...] = w_ref[...].astype(BF16)
    ka = a_ref.shape[1]
    kb = ka + b_ref.shape[1]
    acc = _dot(a_ref[...].astype(BF16), wb_ref[0:ka, :])
    acc = acc + _dot(b_ref[...].astype(BF16), wb_ref[ka:kb, :])
    acc = acc + _dot(c_ref[...].astype(BF16), wb_ref[kb:, :])
    o_ref[...] = x_ref[...] + acc


def _outproj(x, a, b, c, w_out, layer, tm, tn):
    m, d = x.shape
    k = w_out.shape[1]
    return pl.pallas_call(
        _outproj_kernel,
        grid=(d // tn, m // tm),
        in_specs=[pl.BlockSpec((tm, tn), lambda j, i: (i, j)),
                  pl.BlockSpec((tm, a.shape[1]), lambda j, i: (i, 0)),
                  pl.BlockSpec((tm, b.shape[1]), lambda j, i: (i, 0)),
                  pl.BlockSpec((tm, c.shape[1]), lambda j, i: (i, 0)),
                  pl.BlockSpec((None, k, tn), lambda j, i: (layer, 0, j))],
        out_specs=pl.BlockSpec((tm, tn), lambda j, i: (i, j)),
        out_shape=jax.ShapeDtypeStruct((m, d), F32),
        scratch_shapes=[pltpu.VMEM((k, tn), BF16)],
        compiler_params=_cparams(2),
        name="out_proj",
    )(x, a, b, c, w_out)


def _gating_kernel(u_ref, v_ref, z_ref, w_ref, b_ref, o_ref):
    rows = u_ref.shape[0]
    ri = lax.broadcasted_iota(jnp.int32, (CHUNK_MLP, CHUNK_MLP), 0)
    ci = lax.broadcasted_iota(jnp.int32, (CHUNK_MLP, CHUNK_MLP), 1)
    w = jnp.where(ri >= ci, w_ref[0], 0.0).astype(BF16)
    bias = b_ref[0]
    for c in range(rows // CHUNK_MLP):
        sl = slice(c * CHUNK_MLP, (c + 1) * CHUNK_MLP)
        vg = _gelu(v_ref[sl, :]).astype(BF16)
        mixed = _dot(w, vg) + bias
        o_ref[sl, :] = (_gelu(u_ref[sl, :]) * mixed * _silu(z_ref[sl, :])).astype(o_ref.dtype)


def _gating_prompt(proj, chunk_w, bias_b, rows):
    m = proj.shape[0]
    nb = HEAD_DIM
    return pl.pallas_call(
        _gating_kernel,
        grid=(A_GROUPS, m // rows),
        in_specs=[pl.BlockSpec((rows, nb), lambda g, i: (i, OFF_AU // nb + g)),
                  pl.BlockSpec((rows, nb), lambda g, i: (i, OFF_AV // nb + g)),
                  pl.BlockSpec((rows, nb), lambda g, i: (i, OFF_AZ // nb + g)),
                  pl.BlockSpec((1, CHUNK_MLP, CHUNK_MLP), lambda g, i: (g, 0, 0)),
                  pl.BlockSpec((1, CHUNK_MLP, nb), lambda g, i: (g, 0, 0))],
        out_specs=pl.BlockSpec((rows, nb), lambda g, i: (i, g)),
        out_shape=jax.ShapeDtypeStruct((m, A_W), BF16),
        compiler_params=_cparams(2),
        name="gating_prompt",
    )(proj, proj, proj, chunk_w, bias_b)


GDN_HEADS_PER_STEP = 12
GDN_ROWS_PER_STEP = 512
NEUMANN_SPLIT_STEPS = 2


def _split2(a):
    hi = a.astype(BF16)
    lo = (a - hi.astype(F32)).astype(BF16)
    return hi, lo


def _dot3(ah, al, bh, bl):
    return _dot(ah, bh) + (_dot(ah, bl) + _dot(al, bh))


def _gdn_prompt_kernel(q_ref, k_ref, v_ref, z_ref, ab_ref, wq_ref, wk_ref, wv_ref,
                       alog_ref, dtb_ref, gain_ref, o_ref, s_ref, st_ref, tail_ref):
    hg = pl.program_id(1)
    tb = pl.program_id(2)
    rows = q_ref.shape[1]
    heads = GDN_HEADS_PER_STEP
    pair = 2 * GDN_CHUNK

    ri = lax.broadcasted_iota(jnp.int32, (pair, pair), 0)
    ci = lax.broadcasted_iota(jnp.int32, (pair, pair), 1)
    same = (ri >= GDN_CHUNK) == (ci >= GDN_CHUNK)
    incl = same & (ri >= ci)
    strict = same & (ri > ci)
    incl16 = jnp.where(incl, 1.0, 0.0).astype(BF16)
    eye = jnp.where(ri == ci, 1.0, 0.0)
    row_lo = lax.broadcasted_iota(jnp.int32, (pair, HEAD_DIM), 0) < GDN_CHUNK
    gain = gain_ref[...]

    @pl.when(tb == 0)
    def _():
        st_ref[...] = jnp.zeros_like(st_ref)
        tail_ref[...] = jnp.zeros_like(tail_ref)

    hs = range(heads)
    cols = [slice(g * HEAD_DIM, (g + 1) * HEAD_DIM) for g in hs]
    pick_a = [jnp.where(ri == (hg * heads + g), 1.0, 0.0).astype(BF16) for g in hs]
    pick_b = [jnp.where(ri == (B_HEADS + hg * heads + g), 1.0, 0.0).astype(BF16) for g in hs]
    neg_a_row = -jnp.exp(alog_ref[...])
    dtb_row = dtb_ref[...]

    def split3(x):
        hi = x.astype(BF16)
        r = x - hi.astype(F32)
        mid = r.astype(BF16)
        return hi, mid, (r - mid.astype(F32)).astype(BF16)

    def body(i, carry):
        r0 = pl.multiple_of(i * pair, pair)
        rp = pl.multiple_of(jnp.maximum(r0 - 8, 0), 8)
        ab = ab_ref[0, pl.ds(r0, pair), :]

        def conv(x_ref, w_ref, part, g):
            cur = x_ref[0, pl.ds(r0, pair), cols[g]]
            prev = jnp.where(i > 0, x_ref[0, pl.ds(rp, 8), cols[g]], tail_ref[part, :, cols[g]])
            w = w_ref[:, cols[g]]
            head = jnp.concatenate([prev, cur[0:8]], axis=0)
            acc = cur * w[CONV_WIDTH - 1:CONV_WIDTH]
            for s in range(1, CONV_WIDTH):
                shifted = jnp.concatenate([head[8 - s:16 - s], pltpu.roll(cur, s, 0)[8:]], axis=0)
                acc = acc + shifted * w[CONV_WIDTH - 1 - s:CONV_WIDTH - s]
            return _silu(acc)

        def l2n(x):
            return x * lax.rsqrt(jnp.sum(x * x, axis=-1, keepdims=True) + EPS)

        q = [l2n(conv(q_ref, wq_ref, 0, g)) * (HEAD_DIM ** -0.5) for g in hs]
        k = [l2n(conv(k_ref, wk_ref, 1, g)) for g in hs]
        v = [conv(v_ref, wv_ref, 2, g) for g in hs]
        g_all = neg_a_row * jax.nn.softplus(ab + dtb_row)
        gp = split3(g_all)
        gc_all = _dot(incl16, gp[0]) + (_dot(incl16, gp[1]) + _dot(incl16, gp[2]))
        cp = split3(gc_all)
        bp = _split2(jax.nn.sigmoid(ab))
        gcb = [_dot(cp[0], pick_a[g]) + (_dot(cp[1], pick_a[g]) + _dot(cp[2], pick_a[g])) for g in hs]
        beta = [_dot(bp[0], pick_b[g]) + _dot(bp[1], pick_b[g]) for g in hs]
        egc = [jnp.exp(x) for x in gcb]
        glast = [jnp.where(row_lo, x[GDN_CHUNK - 1:GDN_CHUNK, :], x[pair - 1:pair, :]) for x in gcb]
        kdec = [k[g] * jnp.exp(glast[g] - gcb[g]) for g in hs]
        decay = [jnp.where(incl, jnp.exp(jnp.where(incl, x - x.T, 0.0)), 0.0) for x in gcb]

        kb = [k[g] * beta[g] for g in hs]
        k16 = [x.astype(BF16) for x in k]
        lmat = [jnp.where(strict, _dot_nt(kb[g].astype(BF16), k16[g]) * decay[g], 0.0) for g in hs]
        tinv = [eye - x for x in lmat]
        lsp = [_split2(x) for x in lmat]
        pw = [_dot3(lh, ll, lh, ll) for lh, ll in lsp]
        for step in range(5):
            if step < NEUMANN_SPLIT_STEPS:
                psp = [_split2(x) for x in pw]
                tsp = [_split2(x) for x in tinv]
                tinv = [tinv[g] + _dot3(tsp[g][0], tsp[g][1], psp[g][0], psp[g][1]) for g in hs]
                pw = [_dot3(ph, pl_, ph, pl_) for ph, pl_ in psp]
            else:
                p16 = [x.astype(BF16) for x in pw]
                tinv = [tinv[g] + _dot(tinv[g].astype(BF16), p16[g]) for g in hs]
                if step < 4:
                    pw = [_dot(x, x) for x in p16]
        t16 = [x.astype(BF16) for x in tinv]
        u = [_dot(t16[g], (v[g] * beta[g]).astype(BF16)) for g in hs]
        w16 = [_dot(t16[g], (kb[g] * egc[g]).astype(BF16)).astype(BF16) for g in hs]
        a_intra = [(_dot_nt(q[g].astype(BF16), k16[g]) * decay[g]).astype(BF16) for g in hs]
        qdec = [(q[g] * egc[g]).astype(BF16) for g in hs]

        s = [st_ref[g] for g in hs]
        outs = [[] for _ in hs]
        for c in range(2):
            sl = slice(c * GDN_CHUNK, (c + 1) * GDN_CHUNK)
            in_chunk = row_lo if c == 0 else jnp.logical_not(row_lo)
            s16 = [x.astype(BF16) for x in s]
            vn16 = [jnp.where(in_chunk, u[g] - _dot(w16[g], s16[g]), 0.0).astype(BF16) for g in hs]
            for g in hs:
                outs[g].append(_dot(qdec[g][sl], s16[g]) + _dot(a_intra[g][sl], vn16[g]))
            gt = [jnp.exp(x[(c + 1) * GDN_CHUNK - 1:(c + 1) * GDN_CHUNK, :]) for x in gcb]
            kd = [jnp.where(in_chunk, x, 0.0).astype(BF16) for x in kdec]
            s = [s[g] * gt[g] + _dot_tn(kd[g], vn16[g]) for g in hs]
        for g in hs:
            st_ref[g] = s[g]
            o = jnp.concatenate(outs[g], axis=0)
            o = o * lax.rsqrt(jnp.mean(o * o, axis=-1, keepdims=True) + EPS) * gain
            o_ref[0, pl.ds(r0, pair), cols[g]] = (
                o * _silu(z_ref[0, pl.ds(r0, pair), cols[g]])).astype(o_ref.dtype)
        return carry

    lax.fori_loop(0, rows // pair, body, 0)
    for part, x_ref in enumerate((q_ref, k_ref, v_ref)):
        tail_ref[part] = x_ref[0, rows - 8:rows, :]

    @pl.when(tb == pl.num_programs(2) - 1)
    def _():
        s_ref[0] = st_ref[...]


def _gdn_prompt(proj3, ab3, conv_w, alog, dtb, gain):
    bn, t, _ = proj3.shape
    heads = GDN_HEADS_PER_STEP
    rows = min(GDN_ROWS_PER_STEP, t)
    width = heads * HEAD_DIM
    n_hg = B_HEADS // heads
    col = lambda off: (lambda b, hg, tb: (b, tb, off // width + hg))
    wcol = lambda part: (lambda b, hg, tb: (0, part * n_hg + hg))
    vec = pl.BlockSpec((1, HEAD_DIM), lambda b, hg, tb: (0, 0))
    return pl.pallas_call(
        _gdn_prompt_kernel,
        grid=(bn, n_hg, t // rows),
        in_specs=[pl.BlockSpec((1, rows, width), col(OFF_BQ)),
                  pl.BlockSpec((1, rows, width), col(OFF_BK)),
                  pl.BlockSpec((1, rows, width), col(OFF_BV)),
                  pl.BlockSpec((1, rows, width), col(OFF_BZ)),
                  pl.BlockSpec((1, rows, GATE_PAD), lambda b, hg, tb: (b, tb, 0)),
                  pl.BlockSpec((CONV_WIDTH, width), wcol(0)),
                  pl.BlockSpec((CONV_WIDTH, width), wcol(1)),
                  pl.BlockSpec((CONV_WIDTH, width), wcol(2)),
                  vec, vec, vec],
        out_specs=[pl.BlockSpec((1, rows, width), lambda b, hg, tb: (b, tb, hg)),
                   pl.BlockSpec((1, heads, HEAD_DIM, HEAD_DIM), lambda b, hg, tb: (b, hg, 0, 0))],
        out_shape=[jax.ShapeDtypeStruct((bn, t, B_W), BF16),
                   jax.ShapeDtypeStruct((bn, B_HEADS, HEAD_DIM, HEAD_DIM), F32)],
        scratch_shapes=[pltpu.VMEM((heads, HEAD_DIM, HEAD_DIM), F32),
                        pltpu.VMEM((3, 8, width), F32)],
        compiler_params=_cparams(3),
        name="gdn_prompt",
    )(proj3, proj3, proj3, proj3, ab3, conv_w, conv_w, conv_w, alog, dtb, gain)


AUG_LANE0 = QK_DIM
POS_SPLIT = 64
ATTN_BLOCK = 512
ATTN_HEADS_PER_STEP = 1


def _half_rmsnorm(x, gain2):
    lo = lax.broadcasted_iota(jnp.int32, x.shape, x.ndim - 1) < QK_DIM
    x2 = x * x
    s_lo = jnp.sum(jnp.where(lo, x2, 0.0), axis=-1, keepdims=True)
    s_hi = jnp.sum(jnp.where(lo, 0.0, x2), axis=-1, keepdims=True)
    ms = jnp.where(lo, s_lo, s_hi) * (1.0 / QK_DIM)
    return x * lax.rsqrt(ms + EPS) * gain2


def _alibi_tables(slopes_np, t):
    import ml_dtypes
    bf = ml_dtypes.bfloat16
    hi = slopes_np.astype(bf).astype(np.float32)
    mid = (slopes_np - hi).astype(bf).astype(np.float32)
    lo = (slopes_np - hi - mid).astype(bf).astype(np.float32)
    qaug = np.zeros((C_HEADS, HEAD_DIM), np.float32)
    for n, piece in enumerate((hi, mid, lo)):
        qaug[:, AUG_LANE0 + n] = POS_SPLIT * piece
        qaug[:, AUG_LANE0 + 3 + n] = piece
    pos = np.arange(t)
    kaug = np.zeros((t, HEAD_DIM), np.float32)
    kaug[:, AUG_LANE0:AUG_LANE0 + 3] = (pos // POS_SPLIT)[:, None]
    kaug[:, AUG_LANE0 + 3:AUG_LANE0 + 6] = (pos % POS_SPLIT)[:, None]
    return jnp.asarray(qaug.reshape(1, C_W)), jnp.asarray(kaug)


def _cprep_kernel(q_ref, k_ref, v_ref, gq_ref, gk_ref, qaug_ref, kaug_ref, *rest):
    q1_ref, q2_ref, kn_ref, k1_ref, k2_ref, vo_ref, vb_ref = rest[-7:]
    gq = gq_ref[...]
    gk = gk_ref[...]
    kaug = kaug_ref[...]
    lo = lax.broadcasted_iota(jnp.int32, kaug.shape, 1) < QK_DIM
    for h in range(C_HEADS):
        sl = slice(h * HEAD_DIM, (h + 1) * HEAD_DIM)
        qn = _half_rmsnorm(q_ref[0, :, sl], gq) * (QK_DIM ** -0.5)
        kn = _half_rmsnorm(k_ref[0, :, sl], gk)
        qaug = qaug_ref[:, sl]
        q1_ref[0, :, sl] = jnp.where(lo, qn, qaug).astype(BF16)
        q2_ref[0, :, sl] = jnp.where(lo, pltpu.roll(qn, QK_DIM, 1), qaug).astype(BF16)
        kn_ref[0, h] = kn
        k1_ref[0, :, sl] = jnp.where(lo, kn, kaug).astype(BF16)
        k2_ref[0, :, sl] = jnp.where(lo, pltpu.roll(kn, QK_DIM, 1), kaug).astype(BF16)
        vo_ref[0, h] = v_ref[0, :, sl]
    vb_ref[...] = v_ref[...].astype(BF16)


def _cprep_prompt(proj3, gq2, gk2, qaug, kaug, tr, layer, depth, kv_all):
    bn, t, _ = proj3.shape
    blk = lambda off: pl.BlockSpec((1, tr, C_W), lambda b, r: (b, r, off // C_W))
    out = pl.BlockSpec((1, tr, C_W), lambda b, r: (b, r, 0))
    out_hm = pl.BlockSpec((None, 1, C_HEADS, tr, HEAD_DIM), lambda b, r: (layer, b, 0, r, 0))
    vec = pl.BlockSpec((1, HEAD_DIM), lambda b, r: (0, 0))
    sd = lambda dt: jax.ShapeDtypeStruct((bn, t, C_W), dt)
    sd_hm = jax.ShapeDtypeStruct((depth, bn, C_HEADS, t, HEAD_DIM), F32)
    in_specs = [blk(OFF_CQ), blk(OFF_CK), blk(OFF_CV), vec, vec,
                pl.BlockSpec((1, C_W), lambda b, r: (0, 0)),
                pl.BlockSpec((tr, HEAD_DIM), lambda b, r: (r, 0))]
    args = [proj3, proj3, proj3, gq2, gk2, qaug, kaug]
    aliases = {}
    if kv_all is not None:
        in_specs += [pl.BlockSpec(memory_space=pl.ANY)] * 2
        aliases = {len(args): 2, len(args) + 1: 5}
        args += list(kv_all)
    return pl.pallas_call(
        _cprep_kernel,
        grid=(bn, t // tr),
        in_specs=in_specs,
        out_specs=[out, out, out_hm, out, out, out_hm, out],
        out_shape=[sd(BF16), sd(BF16), sd_hm, sd(BF16), sd(BF16), sd_hm, sd(BF16)],
        input_output_aliases=aliases,
        compiler_params=_cparams(2),
        name="cprep_prompt",
    )(*args)


def _lambda_value(lamv, lam_init):
    e1 = jnp.exp(jnp.sum(lamv[0:1] * lamv[1:2], axis=-1, keepdims=True))
    e2 = jnp.exp(jnp.sum(lamv[2:3] * lamv[3:4], axis=-1, keepdims=True))
    return e1 - e2 + lam_init


def _attn_prompt_kernel(q1_ref, q2_ref, k1_ref, k2_ref, v_ref, z_ref, lamv_ref, gain_ref, o_ref,
                        *, lam_init, tq):
    qi = pl.program_id(2)
    heads = q1_ref.shape[2] // HEAD_DIM
    cols = [slice(g * HEAD_DIM, (g + 1) * HEAD_DIM) for g in range(heads)]
    chains = [(g, c) for g in range(heads) for c in range(2)]
    nc = len(chains)
    q = [(q1_ref, q2_ref)[c][0, :, cols[g]] for g, c in chains]
    causal = (lax.broadcasted_iota(jnp.int32, (tq, tq), 1) <= lax.broadcasted_iota(jnp.int32, (tq, tq), 0))

    def step(j, carry, masked):
        m, l, acc = carry[0::3], carry[1::3], carry[2::3]
        c0 = pl.multiple_of(j * tq, tq)
        vblk = [v_ref[0, pl.ds(c0, tq), cols[g]] for g in range(heads)]
        s = [_dot_nt(q[n], (k1_ref, k2_ref)[c][0, pl.ds(c0, tq), cols[g]]) for n, (g, c) in enumerate(chains)]
        if masked:
            s = [jnp.where(causal, x, NEG_BIG) for x in s]
        m_new = [jnp.maximum(m[n], jnp.max(s[n], axis=-1, keepdims=True)) for n in range(nc)]
        alpha = [jnp.exp(m[n] - m_new[n]) for n in range(nc)]
        p = [jnp.exp(s[n] - m_new[n]) for n in range(nc)]
        l = [alpha[n] * l[n] + jnp.sum(p[n], axis=-1, keepdims=True) for n in range(nc)]
        pv = [_dot(p[n].astype(BF16), vblk[g]) for n, (g, c) in enumerate(chains)]
        acc = [alpha[n] * acc[n] + pv[n] for n in range(nc)]
        out = []
        for n in range(nc):
            out += [m_new[n], l[n], acc[n]]
        return tuple(out)

    m0 = jnp.full((tq, 1), NEG_BIG, F32)
    l0 = jnp.zeros((tq, 1), F32)
    a0 = jnp.zeros((tq, HEAD_DIM), F32)
    carry = lax.fori_loop(0, qi, lambda j, c: step(j, c, False), (m0, l0, a0) * nc)
    carry = step(qi, carry, True)
    l, acc = carry[1::3], carry[2::3]
    lam = _lambda_value(lamv_ref[...], lam_init)
    for g in range(heads):
        o = acc[2 * g] / l[2 * g] - lam * (acc[2 * g + 1] / l[2 * g + 1])
        o = o * lax.rsqrt(jnp.mean(o * o, axis=-1, keepdims=True) + EPS) * gain_ref[...] * (1.0 - lam_init)
        o_ref[0, :, cols[g]] = (o * _silu(z_ref[0, :, cols[g]])).astype(o_ref.dtype)


def _attn_prompt(q1, q2, k1, k2, vb, proj3, lamv, gain, lam_init, tq):
    bn, t, _ = q1.shape
    nb = ATTN_HEADS_PER_STEP * HEAD_DIM
    kern = functools.partial(_attn_prompt_kernel, lam_init=lam_init, tq=tq)
    qspec = pl.BlockSpec((1, tq, nb), lambda b, h, i: (b, i, h))
    kspec = pl.BlockSpec((1, t, nb), lambda b, h, i: (b, 0, h))
    return pl.pallas_call(
        kern,
        grid=(bn, C_HEADS // ATTN_HEADS_PER_STEP, t // tq),
        in_specs=[qspec, qspec, kspec, kspec, kspec,
                  pl.BlockSpec((1, tq, nb), lambda b, h, i: (b, i, OFF_CZ // nb + h)),
                  pl.BlockSpec((4, QK_DIM), lambda b, h, i: (0, 0)),
                  pl.BlockSpec((1, HEAD_DIM), lambda b, h, i: (0, 0))],
        out_specs=pl.BlockSpec((1, tq, nb), lambda b, h, i: (b, i, h)),
        out_shape=jax.ShapeDtypeStruct((bn, t, C_W), BF16),
        compiler_params=_cparams(3),
        name="attn_prompt",
    )(q1, q2, k1, k2, vb, proj3, lamv, gain)


def _sample_mix_kernel(p_ref, ab_ref, cbuf_ref, cw_ref, st_ref, w00_ref, b0_ref, alog_ref, dtb_ref,
                       ggain_ref, gq_ref, gk_ref,
                       aout_ref, av_ref, bout_ref, snew_ref, qn_ref, kn_ref, vn_ref, zc_ref):
    nb = HEAD_DIM
    p = p_ref[0]
    av = _gelu(p[:, OFF_AV:OFF_AV + A_W])
    av_ref[0] = av
    mixed = w00_ref[...] * av + b0_ref[...]
    aout_ref[0] = _gelu(p[:, OFF_AU:OFF_AU + A_W]) * mixed * _silu(p[:, OFF_AZ:OFF_AZ + A_W])

    cb = cbuf_ref[0]
    cw = cw_ref[...]
    x = p[:, OFF_BQ:OFF_BQ + 3 * B_W]
    acc = cb[0:1] * cw[0:1] + cb[1:2] * cw[1:2] + cb[2:3] * cw[2:3] + x * cw[3:4]
    act = _silu(acc)
    ab = ab_ref[0]
    g_row = -jnp.exp(alog_ref[...]) * jax.nn.softplus(ab + dtb_ref[...])
    beta_row = jax.nn.sigmoid(ab)
    ggain = ggain_ref[...]
    row8 = lax.broadcasted_iota(jnp.int32, (8, nb), 0)
    for h in range(B_HEADS):
        q = act[:, h * nb:(h + 1) * nb]
        k = act[:, B_W + h * nb:B_W + (h + 1) * nb]
        v = act[:, 2 * B_W + h * nb:2 * B_W + (h + 1) * nb]
        q = q * lax.rsqrt(jnp.sum(q * q, axis=-1, keepdims=True) + EPS) * (nb ** -0.5)
        k = k * lax.rsqrt(jnp.sum(k * k, axis=-1, keepdims=True) + EPS)
        eg = jnp.exp(g_row[:, h:h + 1])
        beta = beta_row[:, B_HEADS + h:B_HEADS + h + 1]
        s = st_ref[0, h]
        lhs = jnp.where(row8 == 0, k * (beta * eg), jnp.where(row8 == 1, q * eg, 0.0))
        rs = _dot_hi(lhs, s)
        v_new = v * beta - rs[0:1]
        o = rs[1:2] + jnp.sum(q * k, axis=-1, keepdims=True) * v_new
        k8 = jnp.where(row8 == 0, k, 0.0)
        v8 = jnp.where(row8 == 0, v_new, 0.0)
        snew_ref[0, h] = s * eg + _dot_tn(k8, v8, precision=HIGHEST)
        o = o * lax.rsqrt(jnp.mean(o * o, axis=-1, keepdims=True) + EPS) * ggain
        bout_ref[0, :, h * nb:(h + 1) * nb] = o * _silu(p[:, OFF_BZ + h * nb:OFF_BZ + (h + 1) * nb])

    gq = gq_ref[...]
    gk = gk_ref[...]
    for h in range(C_HEADS):
        qn_ref[0, h] = _half_rmsnorm(p[:, OFF_CQ + h * nb:OFF_CQ + (h + 1) * nb], gq) * (QK_DIM ** -0.5)
        kn_ref[0, h] = _half_rmsnorm(p[:, OFF_CK + h * nb:OFF_CK + (h + 1) * nb], gk)
        vn_ref[0, h] = p[:, OFF_CV + h * nb:OFF_CV + (h + 1) * nb]
        zc_ref[0, h] = p[:, OFF_CZ + h * nb:OFF_CZ + (h + 1) * nb]


def _sample_mix(proj3, ab3, cbuf, conv_w, state, w00, b0, alog, dtb, ggain, gq2, gk2):
    bn = proj3.shape[0]
    nb = HEAD_DIM
    row = lambda w: pl.BlockSpec((1, 1, w), lambda b: (b, 0, 0))
    full2 = lambda a: pl.BlockSpec(a.shape, lambda b: (0, 0))
    st = pl.BlockSpec((1, B_HEADS, nb, nb), lambda b: (b, 0, 0, 0))
    heads = pl.BlockSpec((1, C_HEADS, 1, nb), lambda b: (b, 0, 0, 0))
    sd = lambda w: jax.ShapeDtypeStruct((bn, 1, w), F32)
    hd = jax.ShapeDtypeStruct((bn, C_HEADS, 1, nb), F32)
    return pl.pallas_call(
        _sample_mix_kernel,
        grid=(bn,),
        in_specs=[row(MAIN_COLS), row(GATE_PAD),
                  pl.BlockSpec((1, CONV_WIDTH - 1, 3 * B_W), lambda b: (b, 0, 0)),
                  full2(conv_w), st, full2(w00), full2(b0), full2(alog), full2(dtb),
                  full2(ggain), full2(gq2), full2(gk2)],
        out_specs=[row(A_W), row(A_W), row(B_W), st, heads, heads, heads, heads],
        out_shape=[sd(A_W), sd(A_W), sd(B_W),
                   jax.ShapeDtypeStruct(state.shape, F32), hd, hd, hd, hd],
        compiler_params=_cparams(1),
        name="sample_mix",
    )(proj3, ab3, cbuf, conv_w, state, w00, b0, alog, dtb, ggain, gq2, gk2)


def _paged_attn_kernel(pt_ref, qn_ref, kn_ref, vn_ref, z_ref, slope_ref, lamv_ref, gain_ref, *rest,
                       lam_init, past_len, n_groups):
    del pt_ref
    n_scratch = 6
    pp = (len(rest) - 1 - n_scratch) // 2
    k_refs = rest[:pp]
    v_refs = rest[pp:2 * pp]
    o_ref = rest[2 * pp]
    sc_ref, mrun_ref, lrun_ref, acc_ref, q3_ref, snew_ref = rest[2 * pp + 1:]
    ph = pl.program_id(1)
    g = pl.program_id(2)
    page = k_refs[0].shape[1]
    sshape = mrun_ref.shape

    @pl.when((ph == 0) & (g == 0))
    def _():
        q = jnp.broadcast_to(qn_ref[0], q3_ref.shape)
        ri = lax.broadcasted_iota(jnp.int32, q3_ref.shape, 1)
        ci = lax.broadcasted_iota(jnp.int32, q3_ref.shape, 2)
        q3_ref[...] = jnp.where((ci // QK_DIM) == ri, q, 0.0)
        mrun_ref[...] = jnp.full(sshape, NEG_BIG, F32)
        lrun_ref[...] = jnp.zeros(sshape, F32)
        acc_ref[...] = jnp.zeros(acc_ref.shape, F32)

    @pl.when(ph == 0)
    def _():
        q3 = q3_ref[...].astype(BF16)
        slope = slope_ref[...]
        tok = lax.broadcasted_iota(jnp.int32, sshape, 2)
        mrun = mrun_ref[...]
        for i in range(pp):
            pg = g * pp + i
            s = _bdot_nt(q3, k_refs[i][...].astype(BF16))
            dist = (past_len - (pg * page + tok)).astype(F32)
            s = s - slope * dist
            sc_ref[pg] = s
            mrun = jnp.maximum(mrun, s)
        mrun_ref[...] = mrun

    @pl.when((ph == 1) & (g == 0))
    def _():
        s_new = jnp.sum(q3_ref[...].astype(BF16).astype(F32) * kn_ref[0], axis=-1, keepdims=True)
        m = jnp.maximum(jnp.max(mrun_ref[...], axis=-1, keepdims=True), s_new)
        mrun_ref[...] = jnp.broadcast_to(m, sshape)
        snew_ref[...] = jnp.broadcast_to(s_new, sshape)

    @pl.when(ph == 1)
    def _():
        m = mrun_ref[...]
        lrun = lrun_ref[...]
        acc = acc_ref[...]
        for i in range(pp):
            pg = g * pp + i
            p = jnp.exp(sc_ref[pg] - m)
            lrun = lrun + p
            acc = acc + _bdot(p.astype(BF16), v_refs[i][...].astype(BF16))
        lrun_ref[...] = lrun
        acc_ref[...] = acc

    @pl.when((ph == 1) & (g == n_groups - 1))
    def _():
        m = mrun_ref[:, :, 0:1]
        p_new = jnp.exp(snew_ref[:, :, 0:1] - m)
        l = jnp.sum(lrun_ref[...], axis=-1, keepdims=True) + p_new
        normed = (acc_ref[...] + p_new * vn_ref[0]) / l
        lam = _lambda_value(lamv_ref[...], lam_init)
        o = normed[:, 0:1, :] - lam * normed[:, 1:2, :]
        o = o * lax.rsqrt(jnp.mean(o * o, axis=-1, keepdims=True) + EPS) * gain_ref[...] * (1.0 - lam_init)
        o_ref[0] = o * _silu(z_ref[0])


def _paged_attn(page_table, qn, kn, vn, zc, cache_k, cache_v, layer, slope3, lamv, gain, lam_init):
    bn, n_pages = page_table.shape
    page = cache_k.shape[3]
    pp = math.gcd(PAGES_PER_STEP, n_pages)
    n_groups = n_pages // pp
    past_len = n_pages * page
    head_row = pl.BlockSpec((1, C_HEADS, 1, HEAD_DIM), lambda b, ph, g, pt: (b, 0, 0, 0))

    def k_spec(i):
        def imap(b, ph, g, pt):
            grp = jnp.where(ph == 0, g, n_groups - 1)
            return (layer, pt[b, grp * pp + i], 0, 0, 0)
        return pl.BlockSpec((None, None, C_HEADS, page, HEAD_DIM), imap)

    def v_spec(i):
        def imap(b, ph, g, pt):
            grp = jnp.where(ph == 0, 0, g)
            return (layer, pt[b, grp * pp + i], 0, 0, 0)
        return pl.BlockSpec((None, None, C_HEADS, page, HEAD_DIM), imap)

    def full(a):
        nd = a.ndim
        return pl.BlockSpec(a.shape, lambda b, ph, g, pt: (0,) * nd)

    sshape = (C_HEADS, SCORE_ROWS, page)
    qshape = (C_HEADS, SCORE_ROWS, HEAD_DIM)
    kern = functools.partial(_paged_attn_kernel, lam_init=lam_init, past_len=past_len, n_groups=n_groups)
    grid_spec = pltpu.PrefetchScalarGridSpec(
        num_scalar_prefetch=1,
        grid=(bn, 2, n_groups),
        in_specs=[head_row, head_row, head_row, head_row, full(slope3), full(lamv), full(gain)]
                 + [k_spec(i) for i in range(pp)] + [v_spec(i) for i in range(pp)],
        out_specs=head_row,
        scratch_shapes=[pltpu.VMEM((n_pages,) + sshape, F32),
                        pltpu.VMEM(sshape, F32),
                        pltpu.VMEM(sshape, F32),
                        pltpu.VMEM(qshape, F32),
                        pltpu.VMEM(qshape, F32),
                        pltpu.VMEM(sshape, F32)])
    return pl.pallas_call(
        kern,
        grid_spec=grid_spec,
        out_shape=jax.ShapeDtypeStruct((bn, C_HEADS, 1, HEAD_DIM), F32),
        compiler_params=_cparams(3),
        name="paged_attn",
    )(page_table, qn, kn, vn, zc, slope3, lamv, gain, *([cache_k] * pp), *([cache_v] * pp))


def _pad_lanes(v, width=GATE_PAD):
    return jnp.pad(v.astype(F32), (0, width - v.shape[0])).reshape(1, width)


def _layer_params(l, norm_gain, chunk_w, chunk_b, gdn_conv_w, gdn_a_log, gdn_dt_bias,
                  gdn_norm_gain, attn_q_norm, attn_k_norm, lq1, lk1, lq2, lk2, subln):
    return dict(
        layer=l, norm_gain=norm_gain[l], chunk_w=chunk_w[l],
        bias_b=jnp.broadcast_to(chunk_b[l][:, :, None], (A_GROUPS, CHUNK_MLP, HEAD_DIM)),
        w00=jnp.repeat(chunk_w[l][:, 0, 0], HEAD_DIM).reshape(1, A_W),
        b0=jnp.repeat(chunk_b[l][:, 0], HEAD_DIM).reshape(1, A_W),
        conv_w=gdn_conv_w[l], alog=_pad_lanes(gdn_a_log[l]), dtb=_pad_lanes(gdn_dt_bias[l]),
        ggain=gdn_norm_gain[l].reshape(1, HEAD_DIM),
        gq2=jnp.tile(attn_q_norm[l], 2).reshape(1, HEAD_DIM),
        gk2=jnp.tile(attn_k_norm[l], 2).reshape(1, HEAD_DIM),
        lamv=jnp.stack([lq1[l], lk1[l], lq2[l], lk2[l]]).astype(F32),
        subln=subln[l].reshape(1, HEAD_DIM),
        lam_init=0.8 - 0.6 * math.exp(-0.3 * l),
    )


MAIN_TN = 512
MAIN_TM = 1024
GATING_ROWS = 2048


def _prompt_layer(x, lp, w_in_t, w_out, qaug, kaug, depth, kv_all):
    bn, t, d = x.shape
    m = bn * t
    x2 = x.reshape(m, d)
    tm = min(MAIN_TM, m)
    layer = lp['layer']
    h = _rmsnorm(x2, lp['norm_gain'], min(512, m))
    proj = _inproj(h, w_in_t, layer, functools.partial(_main_row_start, tn=MAIN_TN), MAIN_COLS, tm, MAIN_TN)
    gates = _inproj(h, w_in_t, layer, lambda j: GATE_SRC, GATE_PAD, tm, GATE_PAD)
    proj3 = proj.reshape(bn, t, MAIN_COLS)
    gates3 = gates.reshape(bn, t, GATE_PAD)

    a_out = _gating_prompt(proj, lp['chunk_w'], lp['bias_b'], min(GATING_ROWS, t))
    b_out, s_fin = _gdn_prompt(proj3, gates3, lp['conv_w'], lp['alog'], lp['dtb'], lp['ggain'])
    q1, q2, kn, k1, k2, vo, vb = _cprep_prompt(proj3, lp['gq2'], lp['gk2'], qaug, kaug, min(256, t),
                                               layer, depth, kv_all)
    c_out = _attn_prompt(q1, q2, k1, k2, vb, proj3, lp['lamv'], lp['subln'], lp['lam_init'],
                         min(ATTN_BLOCK, t))

    y = _outproj(x2, a_out, b_out.reshape(m, B_W), c_out.reshape(m, C_W), w_out, layer, tm, MAIN_TN)
    conv_new = proj3[:, t - (CONV_WIDTH - 1):, OFF_BQ:OFF_BQ + 3 * B_W]
    return y.reshape(bn, t, d), (kn, vo), s_fin, conv_new


def _sample_layer(x, lp, w_in_t, w_out, slope3, state, cbuf, cache_k, cache_v, page_table):
    bn, t, d = x.shape
    x2 = x.reshape(bn, d)
    layer = lp['layer']
    proj = _norm_inproj_small(x2, lp['norm_gain'], w_in_t, layer,
                              functools.partial(_main_row_start, tn=MAIN_TN), MAIN_COLS, MAIN_TN)
    gates = _norm_inproj_small(x2, lp['norm_gain'], w_in_t, layer, lambda j: GATE_SRC, GATE_PAD, GATE_PAD)
    proj3 = proj.reshape(bn, 1, MAIN_COLS)
    gates3 = gates.reshape(bn, 1, GATE_PAD)
    a_out, a_v, b_out, s_new, qn, kn, vn, zc = _sample_mix(
        proj3, gates3, cbuf, lp['conv_w'], state, lp['w00'], lp['b0'], lp['alog'], lp['dtb'],
        lp['ggain'], lp['gq2'], lp['gk2'])
    c_out = _paged_attn(page_table, qn, kn, vn, zc, cache_k, cache_v, layer, slope3,
                        lp['lamv'], lp['subln'], lp['lam_init'])
    y = _outproj(x2, a_out.reshape(bn, A_W), b_out.reshape(bn, B_W), c_out.reshape(bn, C_W),
                 w_out, layer, bn, MAIN_TN)
    conv_new = jnp.concatenate([cbuf[:, 1:], proj3[:, :, OFF_BQ:OFF_BQ + 3 * B_W]], axis=1)
    return (y.reshape(bn, t, d), kn.reshape(bn, 1, C_HEADS, HEAD_DIM), vn.reshape(bn, 1, C_HEADS, HEAD_DIM),
            s_new, conv_new, a_v)


def kernel(x_prompt, x_sample, cache_attn_k, cache_attn_v, state_gdn, state_gdn_conv, page_table,
           norm_gain, w_in, w_out, chunk_w, chunk_b, gdn_conv_w, gdn_a_log, gdn_dt_bias, gdn_norm_gain,
           attn_q_norm, attn_k_norm, lambda_q1, lambda_k1, lambda_q2, lambda_k2, attn_subln_gain):
    depth = w_in.shape[0]
    slopes_np = _alibi_slopes(C_HEADS)
    page = cache_attn_k.shape[2]
    slope3 = jnp.asarray(np.broadcast_to(slopes_np[:, None, None], (C_HEADS, SCORE_ROWS, page)).copy())
    qaug, kaug = _alibi_tables(slopes_np, x_prompt.shape[1])
    w_in_t = jnp.transpose(w_in, (0, 2, 1))
    ck = jnp.transpose(cache_attn_k, (0, 1, 3, 2, 4))
    cv = jnp.transpose(cache_attn_v, (0, 1, 3, 2, 4))

    yp, ys = x_prompt, x_sample
    outs = [[] for _ in range(7)]
    kv_all = None
    for l in range(depth):
        lp = _layer_params(l, norm_gain, chunk_w, chunk_b, gdn_conv_w, gdn_a_log, gdn_dt_bias,
                           gdn_norm_gain, attn_q_norm, attn_k_norm, lambda_q1, lambda_k1, lambda_q2,
                           lambda_k2, attn_subln_gain)
        yp, kv_all, ps, pc = _prompt_layer(yp, lp, w_in_t, w_out, qaug, kaug, depth, kv_all)
        ys, sk, sv, ss, sc, sa = _sample_layer(ys, lp, w_in_t, w_out, slope3, state_gdn[l], state_gdn_conv[l],
                                               ck, cv, page_table)
        for lst, val in zip(outs, (ps, pc, sk, sv, ss, sc, sa)):
            lst.append(val)
    res = [jnp.stack(o) for o in outs]
    pk = jnp.transpose(kv_all[0], (0, 1, 3, 2, 4))
    pv = jnp.transpose(kv_all[1], (0, 1, 3, 2, 4))
    return (yp, ys, pk, pv) + tuple(res)
```

```python
import functools
import math

import jax
import jax.numpy as jnp
import numpy as np
from jax import lax
from jax.experimental import pallas as pl
from jax.experimental.pallas import tpu as pltpu

F32 = jnp.float32
BF16 = jnp.bfloat16
HIGHEST = lax.Precision.HIGHEST

HEAD_DIM = 128
A_GROUPS = 8
B_HEADS = 12
C_HEADS = 12
A_W = A_GROUPS * HEAD_DIM
B_W = B_HEADS * HEAD_DIM
C_W = C_HEADS * HEAD_DIM
QK_DIM = HEAD_DIM // 2
CHUNK_MLP = 128
GDN_CHUNK = 64
CONV_WIDTH = 4
EPS = 1e-6
INV_SQRT2 = 0.7071067811865476
NEG_BIG = -1e30

OFF_AU, OFF_AV, OFF_AZ = 0, A_W, 2 * A_W
OFF_BQ = 3 * A_W
OFF_BK = OFF_BQ + B_W
OFF_BV = OFF_BK + B_W
OFF_BZ = OFF_BV + B_W
OFF_CQ = OFF_BZ + B_W
OFF_CK = OFF_CQ + C_W
OFF_CV = OFF_CK + C_W
OFF_CZ = OFF_CV + C_W
MAIN_COLS = OFF_CZ + C_W
GATE_SRC = 3 * A_W + 4 * B_W
GATE_PAD = 128

VMEM_LIMIT = 56 * 1024 * 1024
PAGES_PER_STEP = 8
SCORE_ROWS = 8


def _cparams(n_axes):
    return pltpu.CompilerParams(dimension_semantics=("arbitrary",) * n_axes,
                                vmem_limit_bytes=VMEM_LIMIT)


def _gelu(x):
    return 0.5 * x * (1.0 + lax.erf(x * INV_SQRT2))


def _silu(x):
    return x * jax.nn.sigmoid(x)


def _dot(a, b):
    return jnp.dot(a, b, preferred_element_type=F32)


def _dot_hi(a, b):
    return jnp.dot(a, b, preferred_element_type=F32, precision=HIGHEST)


def _dot_nt(a, b):
    return lax.dot_general(a, b, (((1,), (1,)), ((), ())), preferred_element_type=F32)


def _dot_tn(a, b, precision=None):
    return lax.dot_general(a, b, (((0,), (0,)), ((), ())), preferred_element_type=F32,
                           precision=precision)


def _bdot_nt(a, b):
    return lax.dot_general(a, b, (((2,), (2,)), ((0,), (0,))), preferred_element_type=F32)


def _bdot(a, b):
    return lax.dot_general(a, b, (((2,), (1,)), ((0,), (0,))), preferred_element_type=F32)


def _alibi_slopes(n):
    def pow2(m):
        start = 2.0 ** (-8.0 / m)
        return [start ** (i + 1) for i in range(m)]
    p = 2 ** int(math.floor(math.log2(n)))
    s = pow2(p)
    if p < n:
        s = s + pow2(2 * p)[0::2][: n - p]
    return np.array(s, dtype=np.float32)


def _rmsnorm_kernel(x_ref, g_ref, o_ref):
    x = x_ref[...]
    ms = jnp.mean(x * x, axis=-1, keepdims=True)
    o_ref[...] = (x * lax.rsqrt(ms + EPS) * g_ref[...]).astype(o_ref.dtype)


def _rmsnorm(x, gain, tm):
    m, d = x.shape
    return pl.pallas_call(
        _rmsnorm_kernel,
        grid=(m // tm,),
        in_specs=[pl.BlockSpec((tm, d), lambda i: (i, 0)),
                  pl.BlockSpec((1, d), lambda i: (0, 0))],
        out_specs=pl.BlockSpec((tm, d), lambda i: (i, 0)),
        out_shape=jax.ShapeDtypeStruct((m, d), BF16),
        compiler_params=_cparams(1),
        name="rmsnorm",
    )(x, gain.reshape(1, d))


def _main_row_start(j, tn):
    return pl.multiple_of(j * tn + jnp.where(j * tn >= GATE_SRC, 2 * B_HEADS, 0), 8)


def _wt_spec(layer, tn, k, row_start):
    return pl.BlockSpec((pl.Element(1), pl.Element(tn), pl.Element(k)),
                        lambda j, i: (layer, row_start(j), 0))


def _inproj_kernel(a_ref, w_ref, xs_ref, g_ref, o_ref, os_ref, wt_ref):
    @pl.when(pl.program_id(1) == 0)
    def _():
        wt_ref[...] = w_ref[0].T.astype(BF16)
        xs = xs_ref[...]
        ms = jnp.mean(xs * xs, axis=-1, keepdims=True)
        hs = (xs * lax.rsqrt(ms + EPS) * g_ref[...]).astype(BF16)
        os_ref[...] = _dot(hs, wt_ref[...])
    o_ref[...] = _dot(a_ref[...], wt_ref[...])


def _inproj(a, xs, gain, w_t, layer, row_start, n_out, tm, tn):
    m, k = a.shape
    ms = xs.shape[0]
    return pl.pallas_call(
        _inproj_kernel,
        grid=(n_out // tn, m // tm),
        in_specs=[pl.BlockSpec((tm, k), lambda j, i: (i, 0)),
                  _wt_spec(layer, tn, k, row_start),
                  pl.BlockSpec((ms, k), lambda j, i: (0, 0)),
                  pl.BlockSpec((1, k), lambda j, i: (0, 0))],
        out_specs=[pl.BlockSpec((tm, tn), lambda j, i: (i, j)),
                   pl.BlockSpec((ms, tn), lambda j, i: (0, j))],
        out_shape=[jax.ShapeDtypeStruct((m, n_out), F32),
                   jax.ShapeDtypeStruct((ms, n_out), F32)],
        scratch_shapes=[pltpu.VMEM((k, tn), BF16)],
        compiler_params=_cparams(2),
        name="in_proj",
    )(a, w_t, xs, gain.reshape(1, k))


def _mix_dot(a_ref, b_ref, c_ref, wb_ref):
    ka = a_ref.shape[1]
    kb = ka + b_ref.shape[1]
    acc = _dot(a_ref[...].astype(BF16), wb_ref[0:ka, :])
    acc = acc + _dot(b_ref[...].astype(BF16), wb_ref[ka:kb, :])
    return acc + _dot(c_ref[...].astype(BF16), wb_ref[kb:, :])


def _outproj_kernel(x_ref, a_ref, b_ref, c_ref, w_ref, xs_ref, as_ref, bs_ref, cs_ref,
                    o_ref, os_ref, wb_ref):
    @pl.when(pl.program_id(1) == 0)
    def _():
        wb_ref[...] = w_ref[...].astype(BF16)
        os_ref[...] = xs_ref[...] + _mix_dot(as_ref, bs_ref, cs_ref, wb_ref)
    o_ref[...] = x_ref[...] + _mix_dot(a_ref, b_ref, c_ref, wb_ref)


def _outproj(x, a, b, c, xs, a_s, b_s, c_s, w_out, layer, tm, tn):
    m, d = x.shape
    ms = xs.shape[0]
    k = w_out.shape[1]
    rows = lambda arr: pl.BlockSpec((tm, arr.shape[1]), lambda j, i: (i, 0))
    whole = lambda arr: pl.BlockSpec(arr.shape, lambda j, i: (0, 0))
    return pl.pallas_call(
        _outproj_kernel,
        grid=(d // tn, m // tm),
        in_specs=[pl.BlockSpec((tm, tn), lambda j, i: (i, j)), rows(a), rows(b), rows(c),
                  pl.BlockSpec((None, k, tn), lambda j, i: (layer, 0, j)),
                  pl.BlockSpec((ms, tn), lambda j, i: (0, j)), whole(a_s), whole(b_s), whole(c_s)],
        out_specs=[pl.BlockSpec((tm, tn), lambda j, i: (i, j)),
                   pl.BlockSpec((ms, tn), lambda j, i: (0, j))],
        out_shape=[jax.ShapeDtypeStruct((m, d), F32), jax.ShapeDtypeStruct((ms, d), F32)],
        scratch_shapes=[pltpu.VMEM((k, tn), BF16)],
        compiler_params=_cparams(2),
        name="out_proj",
    )(x, a, b, c, w_out, xs, a_s, b_s, c_s)


def _gating_kernel(u_ref, v_ref, z_ref, w_ref, b_ref, o_ref):
    rows = u_ref.shape[0]
    ri = lax.broadcasted_iota(jnp.int32, (CHUNK_MLP, CHUNK_MLP), 0)
    ci = lax.broadcasted_iota(jnp.int32, (CHUNK_MLP, CHUNK_MLP), 1)
    w = jnp.where(ri >= ci, w_ref[0], 0.0).astype(BF16)
    bias = b_ref[0]
    for c in range(rows // CHUNK_MLP):
        sl = slice(c * CHUNK_MLP, (c + 1) * CHUNK_MLP)
        vg = _gelu(v_ref[sl, :]).astype(BF16)
        mixed = _dot(w, vg) + bias
        o_ref[sl, :] = (_gelu(u_ref[sl, :]) * mixed * _silu(z_ref[sl, :])).astype(o_ref.dtype)


def _gating_prompt(proj, chunk_w, bias_b, rows):
    m = proj.shape[0]
    nb = HEAD_DIM
    return pl.pallas_call(
        _gating_kernel,
        grid=(A_GROUPS, m // rows),
        in_specs=[pl.BlockSpec((rows, nb), lambda g, i: (i, OFF_AU // nb + g)),
                  pl.BlockSpec((rows, nb), lambda g, i: (i, OFF_AV // nb + g)),
                  pl.BlockSpec((rows, nb), lambda g, i: (i, OFF_AZ // nb + g)),
                  pl.BlockSpec((1, CHUNK_MLP, CHUNK_MLP), lambda g, i: (g, 0, 0)),
                  pl.BlockSpec((1, CHUNK_MLP, nb), lambda g, i: (g, 0, 0))],
        out_specs=pl.BlockSpec((rows, nb), lambda g, i: (i, g)),
        out_shape=jax.ShapeDtypeStruct((m, A_W), BF16),
        compiler_params=_cparams(2),
        name="gating_prompt",
    )(proj, proj, proj, chunk_w, bias_b)


GDN_HEADS_PER_STEP = 12
GDN_ROWS_PER_STEP = 512
NEUMANN_SPLIT_STEPS = 2


def _split2(a):
    hi = a.astype(BF16)
    lo = (a - hi.astype(F32)).astype(BF16)
    return hi, lo


def _dot3(ah, al, bh, bl):
    return _dot(ah, bh) + (_dot(ah, bl) + _dot(al, bh))


def _gdn_prompt_kernel(q_ref, k_ref, v_ref, z_ref, ab_ref, wq_ref, wk_ref, wv_ref,
                       alog_ref, dtb_ref, gain_ref, o_ref, s_ref, st_ref, tail_ref):
    hg = pl.program_id(1)
    tb = pl.program_id(2)
    rows = q_ref.shape[1]
    heads = GDN_HEADS_PER_STEP
    pair = 2 * GDN_CHUNK

    ri = lax.broadcasted_iota(jnp.int32, (pair, pair), 0)
    ci = lax.broadcasted_iota(jnp.int32, (pair, pair), 1)
    same = (ri >= GDN_CHUNK) == (ci >= GDN_CHUNK)
    incl = same & (ri >= ci)
    strict = same & (ri > ci)
    incl16 = jnp.where(incl, 1.0, 0.0).astype(BF16)
    eye = jnp.where(ri == ci, 1.0, 0.0)
    row_lo = lax.broadcasted_iota(jnp.int32, (pair, HEAD_DIM), 0) < GDN_CHUNK
    gain = gain_ref[...]

    @pl.when(tb == 0)
    def _():
        st_ref[...] = jnp.zeros_like(st_ref)
        tail_ref[...] = jnp.zeros_like(tail_ref)

    hs = range(heads)
    cols = [slice(g * HEAD_DIM, (g + 1) * HEAD_DIM) for g in hs]
    pick_a = [jnp.where(ri == (hg * heads + g), 1.0, 0.0).astype(BF16) for g in hs]
    pick_b = [jnp.where(ri == (B_HEADS + hg * heads + g), 1.0, 0.0).astype(BF16) for g in hs]
    neg_a_row = -jnp.exp(alog_ref[...])
    dtb_row = dtb_ref[...]

    def split3(x):
        hi = x.astype(BF16)
        r = x - hi.astype(F32)
        mid = r.astype(BF16)
        return hi, mid, (r - mid.astype(F32)).astype(BF16)

    def body(i, carry):
        r0 = pl.multiple_of(i * pair, pair)
        rp = pl.multiple_of(jnp.maximum(r0 - 8, 0), 8)
        ab = ab_ref[0, pl.ds(r0, pair), :]

        def conv(x_ref, w_ref, part, g):
            cur = x_ref[0, pl.ds(r0, pair), cols[g]]
            prev = jnp.where(i > 0, x_ref[0, pl.ds(rp, 8), cols[g]], tail_ref[part, :, cols[g]])
            w = w_ref[:, cols[g]]
            head = jnp.concatenate([prev, cur[0:8]], axis=0)
            acc = cur * w[CONV_WIDTH - 1:CONV_WIDTH]
            for s in range(1, CONV_WIDTH):
                shifted = jnp.concatenate([head[8 - s:16 - s], pltpu.roll(cur, s, 0)[8:]], axis=0)
                acc = acc + shifted * w[CONV_WIDTH - 1 - s:CONV_WIDTH - s]
            return _silu(acc)

        def l2n(x):
            return x * lax.rsqrt(jnp.sum(x * x, axis=-1, keepdims=True) + EPS)

        q = [l2n(conv(q_ref, wq_ref, 0, g)) * (HEAD_DIM ** -0.5) for g in hs]
        k = [l2n(conv(k_ref, wk_ref, 1, g)) for g in hs]
        v = [conv(v_ref, wv_ref, 2, g) for g in hs]
        g_all = neg_a_row * jax.nn.softplus(ab + dtb_row)
        gp = split3(g_all)
        gc_all = _dot(incl16, gp[0]) + (_dot(incl16, gp[1]) + _dot(incl16, gp[2]))
        cp = split3(gc_all)
        bp = _split2(jax.nn.sigmoid(ab))
        gcb = [_dot(cp[0], pick_a[g]) + (_dot(cp[1], pick_a[g]) + _dot(cp[2], pick_a[g])) for g in hs]
        beta = [_dot(bp[0], pick_b[g]) + _dot(bp[1], pick_b[g]) for g in hs]
        egc = [jnp.exp(x) for x in gcb]
        glast = [jnp.where(row_lo, x[GDN_CHUNK - 1:GDN_CHUNK, :], x[pair - 1:pair, :]) for x in gcb]
        kdec = [k[g] * jnp.exp(glast[g] - gcb[g]) for g in hs]
        decay = [jnp.where(incl, jnp.exp(jnp.where(incl, x - x.T, 0.0)), 0.0) for x in gcb]

        kb = [k[g] * beta[g] for g in hs]
        k16 = [x.astype(BF16) for x in k]
        lmat = [jnp.where(strict, _dot_nt(kb[g].astype(BF16), k16[g]) * decay[g], 0.0) for g in hs]
        tinv = [eye - x for x in lmat]
        lsp = [_split2(x) for x in lmat]
        pw = [_dot3(lh, ll, lh, ll) for lh, ll in lsp]
        for step in range(5):
            if step < NEUMANN_SPLIT_STEPS:
                psp = [_split2(x) for x in pw]
                tsp = [_split2(x) for x in tinv]
                tinv = [tinv[g] + _dot3(tsp[g][0], tsp[g][1], psp[g][0], psp[g][1]) for g in hs]
                pw = [_dot3(ph, pl_, ph, pl_) for ph, pl_ in psp]
            else:
                p16 = [x.astype(BF16) for x in pw]
                tinv = [tinv[g] + _dot(tinv[g].astype(BF16), p16[g]) for g in hs]
                if step < 4:
                    pw = [_dot(x, x) for x in p16]
        t16 = [x.astype(BF16) for x in tinv]
        u = [_dot(t16[g], (v[g] * beta[g]).astype(BF16)) for g in hs]
        w16 = [_dot(t16[g], (kb[g] * egc[g]).astype(BF16)).astype(BF16) for g in hs]
        a_intra = [(_dot_nt(q[g].astype(BF16), k16[g]) * decay[g]).astype(BF16) for g in hs]
        qdec = [(q[g] * egc[g]).astype(BF16) for g in hs]

        s = [st_ref[g] for g in hs]
        outs = [[] for _ in hs]
        for c in range(2):
            sl = slice(c * GDN_CHUNK, (c + 1) * GDN_CHUNK)
            in_chunk = row_lo if c == 0 else jnp.logical_not(row_lo)
            s16 = [x.astype(BF16) for x in s]
            vn16 = [jnp.where(in_chunk, u[g] - _dot(w16[g], s16[g]), 0.0).astype(BF16) for g in hs]
            for g in hs:
                outs[g].append(_dot(qdec[g][sl], s16[g]) + _dot(a_intra[g][sl], vn16[g]))
            gt = [jnp.exp(x[(c + 1) * GDN_CHUNK - 1:(c + 1) * GDN_CHUNK, :]) for x in gcb]
            kd = [jnp.where(in_chunk, x, 0.0).astype(BF16) for x in kdec]
            s = [s[g] * gt[g] + _dot_tn(kd[g], vn16[g]) for g in hs]
        for g in hs:
            st_ref[g] = s[g]
            o = jnp.concatenate(outs[g], axis=0)
            o = o * lax.rsqrt(jnp.mean(o * o, axis=-1, keepdims=True) + EPS) * gain
            o_ref[0, pl.ds(r0, pair), cols[g]] = (
                o * _silu(z_ref[0, pl.ds(r0, pair), cols[g]])).astype(o_ref.dtype)
        return carry

    lax.fori_loop(0, rows // pair, body, 0)
    for part, x_ref in enumerate((q_ref, k_ref, v_ref)):
        tail_ref[part] = x_ref[0, rows - 8:rows, :]

    @pl.when(tb == pl.num_programs(2) - 1)
    def _():
        s_ref[0] = st_ref[...]


def _gdn_prompt(proj3, ab3, conv_w, alog, dtb, gain):
    bn, t, _ = proj3.shape
    heads = GDN_HEADS_PER_STEP
    rows = min(GDN_ROWS_PER_STEP, t)
    width = heads * HEAD_DIM
    n_hg = B_HEADS // heads
    col = lambda off: (lambda b, hg, tb: (b, tb, off // width + hg))
    wcol = lambda part: (lambda b, hg, tb: (0, part * n_hg + hg))
    vec = pl.BlockSpec((1, HEAD_DIM), lambda b, hg, tb: (0, 0))
    return pl.pallas_call(
        _gdn_prompt_kernel,
        grid=(bn, n_hg, t // rows),
        in_specs=[pl.BlockSpec((1, rows, width), col(OFF_BQ)),
                  pl.BlockSpec((1, rows, width), col(OFF_BK)),
                  pl.BlockSpec((1, rows, width), col(OFF_BV)),
                  pl.BlockSpec((1, rows, width), col(OFF_BZ)),
                  pl.BlockSpec((1, rows, GATE_PAD), lambda b, hg, tb: (b, tb, 0)),
                  pl.BlockSpec((CONV_WIDTH, width), wcol(0)),
                  pl.BlockSpec((CONV_WIDTH, width), wcol(1)),
                  pl.BlockSpec((CONV_WIDTH, width), wcol(2)),
                  vec, vec, vec],
        out_specs=[pl.BlockSpec((1, rows, width), lambda b, hg, tb: (b, tb, hg)),
                   pl.BlockSpec((1, heads, HEAD_DIM, HEAD_DIM), lambda b, hg, tb: (b, hg, 0, 0))],
        out_shape=[jax.ShapeDtypeStruct((bn, t, B_W), BF16),
                   jax.ShapeDtypeStruct((bn, B_HEADS, HEAD_DIM, HEAD_DIM), F32)],
        scratch_shapes=[pltpu.VMEM((heads, HEAD_DIM, HEAD_DIM), F32),
                        pltpu.VMEM((3, 8, width), F32)],
        compiler_params=_cparams(3),
        name="gdn_prompt",
    )(proj3, proj3, proj3, proj3, ab3, conv_w, conv_w, conv_w, alog, dtb, gain)


AUG_LANE0 = QK_DIM
POS_SPLIT = 64
ATTN_BLOCK = 512
ATTN_HEADS_PER_STEP = 1


def _half_rmsnorm(x, gain2):
    lo = lax.broadcasted_iota(jnp.int32, x.shape, x.ndim - 1) < QK_DIM
    x2 = x * x
    s_lo = jnp.sum(jnp.where(lo, x2, 0.0), axis=-1, keepdims=True)
    s_hi = jnp.sum(jnp.where(lo, 0.0, x2), axis=-1, keepdims=True)
    ms = jnp.where(lo, s_lo, s_hi) * (1.0 / QK_DIM)
    return x * lax.rsqrt(ms + EPS) * gain2


def _alibi_tables(slopes_np, t):
    import ml_dtypes
    bf = ml_dtypes.bfloat16
    hi = slopes_np.astype(bf).astype(np.float32)
    mid = (slopes_np - hi).astype(bf).astype(np.float32)
    lo = (slopes_np - hi - mid).astype(bf).astype(np.float32)
    qaug = np.zeros((C_HEADS, HEAD_DIM), np.float32)
    for n, piece in enumerate((hi, mid, lo)):
        qaug[:, AUG_LANE0 + n] = POS_SPLIT * piece
        qaug[:, AUG_LANE0 + 3 + n] = piece
    pos = np.arange(t)
    kaug = np.zeros((t, HEAD_DIM), np.float32)
    kaug[:, AUG_LANE0:AUG_LANE0 + 3] = (pos // POS_SPLIT)[:, None]
    kaug[:, AUG_LANE0 + 3:AUG_LANE0 + 6] = (pos % POS_SPLIT)[:, None]
    return jnp.asarray(qaug.reshape(1, C_W)), jnp.asarray(kaug)


def _cprep_kernel(q_ref, k_ref, v_ref, gq_ref, gk_ref, qaug_ref, kaug_ref, *rest):
    q1_ref, q2_ref, kn_ref, k1_ref, k2_ref, vo_ref, vb_ref = rest[-7:]
    gq = gq_ref[...]
    gk = gk_ref[...]
    kaug = kaug_ref[...]
    lo = lax.broadcasted_iota(jnp.int32, kaug.shape, 1) < QK_DIM
    for h in range(C_HEADS):
        sl = slice(h * HEAD_DIM, (h + 1) * HEAD_DIM)
        qn = _half_rmsnorm(q_ref[0, :, sl], gq) * (QK_DIM ** -0.5)
        kn = _half_rmsnorm(k_ref[0, :, sl], gk)
        qaug = qaug_ref[:, sl]
        q1_ref[0, :, sl] = jnp.where(lo, qn, qaug).astype(BF16)
        q2_ref[0, :, sl] = jnp.where(lo, pltpu.roll(qn, QK_DIM, 1), qaug).astype(BF16)
        kn_ref[0, h] = kn
        k1_ref[0, :, sl] = jnp.where(lo, kn, kaug).astype(BF16)
        k2_ref[0, :, sl] = jnp.where(lo, pltpu.roll(kn, QK_DIM, 1), kaug).astype(BF16)
        vo_ref[0, h] = v_ref[0, :, sl]
    vb_ref[...] = v_ref[...].astype(BF16)


def _cprep_prompt(proj3, gq2, gk2, qaug, kaug, tr, layer, depth, kv_all):
    bn, t, _ = proj3.shape
    blk = lambda off: pl.BlockSpec((1, tr, C_W), lambda b, r: (b, r, off // C_W))
    out = pl.BlockSpec((1, tr, C_W), lambda b, r: (b, r, 0))
    out_hm = pl.BlockSpec((None, 1, C_HEADS, tr, HEAD_DIM), lambda b, r: (layer, b, 0, r, 0))
    vec = pl.BlockSpec((1, HEAD_DIM), lambda b, r: (0, 0))
    sd = lambda dt: jax.ShapeDtypeStruct((bn, t, C_W), dt)
    sd_hm = jax.ShapeDtypeStruct((depth, bn, C_HEADS, t, HEAD_DIM), F32)
    in_specs = [blk(OFF_CQ), blk(OFF_CK), blk(OFF_CV), vec, vec,
                pl.BlockSpec((1, C_W), lambda b, r: (0, 0)),
                pl.BlockSpec((tr, HEAD_DIM), lambda b, r: (r, 0))]
    args = [proj3, proj3, proj3, gq2, gk2, qaug, kaug]
    aliases = {}
    if kv_all is not None:
        in_specs += [pl.BlockSpec(memory_space=pl.ANY)] * 2
        aliases = {len(args): 2, len(args) + 1: 5}
        args += list(kv_all)
    return pl.pallas_call(
        _cprep_kernel,
        grid=(bn, t // tr),
        in_specs=in_specs,
        out_specs=[out, out, out_hm, out, out, out_hm, out],
        out_shape=[sd(BF16), sd(BF16), sd_hm, sd(BF16), sd(BF16), sd_hm, sd(BF16)],
        input_output_aliases=aliases,
        compiler_params=_cparams(2),
        name="cprep_prompt",
    )(*args)


def _lambda_value(lamv, lam_init):
    e1 = jnp.exp(jnp.sum(lamv[0:1] * lamv[1:2], axis=-1, keepdims=True))
    e2 = jnp.exp(jnp.sum(lamv[2:3] * lamv[3:4], axis=-1, keepdims=True))
    return e1 - e2 + lam_init


def _attn_prompt_kernel(q1_ref, q2_ref, k1_ref, k2_ref, v_ref, z_ref, lamv_ref, gain_ref, o_ref,
                        *, lam_init, tq):
    qi = pl.program_id(2)
    heads = q1_ref.shape[2] // HEAD_DIM
    cols = [slice(g * HEAD_DIM, (g + 1) * HEAD_DIM) for g in range(heads)]
    chains = [(g, c) for g in range(heads) for c in range(2)]
    nc = len(chains)
    q = [(q1_ref, q2_ref)[c][0, :, cols[g]] for g, c in chains]
    causal = (lax.broadcasted_iota(jnp.int32, (tq, tq), 1) <= lax.broadcasted_iota(jnp.int32, (tq, tq), 0))

    def step(j, carry, masked):
        m, l, acc = carry[0::3], carry[1::3], carry[2::3]
        c0 = pl.multiple_of(j * tq, tq)
        vblk = [v_ref[0, pl.ds(c0, tq), cols[g]] for g in range(heads)]
        s = [_dot_nt(q[n], (k1_ref, k2_ref)[c][0, pl.ds(c0, tq), cols[g]]) for n, (g, c) in enumerate(chains)]
        if masked:
            s = [jnp.where(causal, x, NEG_BIG) for x in s]
        m_new = [jnp.maximum(m[n], jnp.max(s[n], axis=-1, keepdims=True)) for n in range(nc)]
        alpha = [jnp.exp(m[n] - m_new[n]) for n in range(nc)]
        p = [jnp.exp(s[n] - m_new[n]) for n in range(nc)]
        l = [alpha[n] * l[n] + jnp.sum(p[n], axis=-1, keepdims=True) for n in range(nc)]
        pv = [_dot(p[n].astype(BF16), vblk[g]) for n, (g, c) in enumerate(chains)]
        acc = [alpha[n] * acc[n] + pv[n] for n in range(nc)]
        out = []
        for n in range(nc):
            out += [m_new[n], l[n], acc[n]]
        return tuple(out)

    m0 = jnp.full((tq, 1), NEG_BIG, F32)
    l0 = jnp.zeros((tq, 1), F32)
    a0 = jnp.zeros((tq, HEAD_DIM), F32)
    carry = lax.fori_loop(0, qi, lambda j, c: step(j, c, False), (m0, l0, a0) * nc)
    carry = step(qi, carry, True)
    l, acc = carry[1::3], carry[2::3]
    lam = _lambda_value(lamv_ref[...], lam_init)
    for g in range(heads):
        o = acc[2 * g] / l[2 * g] - lam * (acc[2 * g + 1] / l[2 * g + 1])
        o = o * lax.rsqrt(jnp.mean(o * o, axis=-1, keepdims=True) + EPS) * gain_ref[...] * (1.0 - lam_init)
        o_ref[0, :, cols[g]] = (o * _silu(z_ref[0, :, cols[g]])).astype(o_ref.dtype)


def _attn_prompt(q1, q2, k1, k2, vb, proj3, lamv, gain, lam_init, tq):
    bn, t, _ = q1.shape
    nb = ATTN_HEADS_PER_STEP * HEAD_DIM
    kern = functools.partial(_attn_prompt_kernel, lam_init=lam_init, tq=tq)
    qspec = pl.BlockSpec((1, tq, nb), lambda b, h, i: (b, i, h))
    kspec = pl.BlockSpec((1, t, nb), lambda b, h, i: (b, 0, h))
    return pl.pallas_call(
        kern,
        grid=(bn, C_HEADS // ATTN_HEADS_PER_STEP, t // tq),
        in_specs=[qspec, qspec, kspec, kspec, kspec,
                  pl.BlockSpec((1, tq, nb), lambda b, h, i: (b, i, OFF_CZ // nb + h)),
                  pl.BlockSpec((4, QK_DIM), lambda b, h, i: (0, 0)),
                  pl.BlockSpec((1, HEAD_DIM), lambda b, h, i: (0, 0))],
        out_specs=pl.BlockSpec((1, tq, nb), lambda b, h, i: (b, i, h)),
        out_shape=jax.ShapeDtypeStruct((bn, t, C_W), BF16),
        compiler_params=_cparams(3),
        name="attn_prompt",
    )(q1, q2, k1, k2, vb, proj3, lamv, gain)


def _sample_mix_kernel(p_ref, ab_ref, cbuf_ref, cw_ref, st_ref, w00_ref, b0_ref, alog_ref, dtb_ref,
                       ggain_ref, gq_ref, gk_ref,
                       aout_ref, av_ref, bout_ref, snew_ref, qn_ref, kn_ref, vn_ref, zc_ref):
    nb = HEAD_DIM
    p = p_ref[0]
    av = _gelu(p[:, OFF_AV:OFF_AV + A_W])
    av_ref[0] = av
    mixed = w00_ref[...] * av + b0_ref[...]
    aout_ref[0] = _gelu(p[:, OFF_AU:OFF_AU + A_W]) * mixed * _silu(p[:, OFF_AZ:OFF_AZ + A_W])

    cb = cbuf_ref[0]
    cw = cw_ref[...]
    x = p[:, OFF_BQ:OFF_BQ + 3 * B_W]
    acc = cb[0:1] * cw[0:1] + cb[1:2] * cw[1:2] + cb[2:3] * cw[2:3] + x * cw[3:4]
    act = _silu(acc)
    ab = ab_ref[0]
    g_row = -jnp.exp(alog_ref[...]) * jax.nn.softplus(ab + dtb_ref[...])
    beta_row = jax.nn.sigmoid(ab)
    ggain = ggain_ref[...]
    row8 = lax.broadcasted_iota(jnp.int32, (8, nb), 0)
    for h in range(B_HEADS):
        q = act[:, h * nb:(h + 1) * nb]
        k = act[:, B_W + h * nb:B_W + (h + 1) * nb]
        v = act[:, 2 * B_W + h * nb:2 * B_W + (h + 1) * nb]
        q = q * lax.rsqrt(jnp.sum(q * q, axis=-1, keepdims=True) + EPS) * (nb ** -0.5)
        k = k * lax.rsqrt(jnp.sum(k * k, axis=-1, keepdims=True) + EPS)
        eg = jnp.exp(g_row[:, h:h + 1])
        beta = beta_row[:, B_HEADS + h:B_HEADS + h + 1]
        s = st_ref[0, h]
        lhs = jnp.where(row8 == 0, k * (beta * eg), jnp.where(row8 == 1, q * eg, 0.0))
        rs = _dot_hi(lhs, s)
        v_new = v * beta - rs[0:1]
        o = rs[1:2] + jnp.sum(q * k, axis=-1, keepdims=True) * v_new
        k8 = jnp.where(row8 == 0, k, 0.0)
        v8 = jnp.where(row8 == 0, v_new, 0.0)
        snew_ref[0, h] = s * eg + _dot_tn(k8, v8, precision=HIGHEST)
        o = o * lax.rsqrt(jnp.mean(o * o, axis=-1, keepdims=True) + EPS) * ggain
        bout_ref[0, :, h * nb:(h + 1) * nb] = o * _silu(p[:, OFF_BZ + h * nb:OFF_BZ + (h + 1) * nb])

    gq = gq_ref[...]
    gk = gk_ref[...]
    for h in range(C_HEADS):
        qn_ref[0, h] = _half_rmsnorm(p[:, OFF_CQ + h * nb:OFF_CQ + (h + 1) * nb], gq) * (QK_DIM ** -0.5)
        kn_ref[0, h] = _half_rmsnorm(p[:, OFF_CK + h * nb:OFF_CK + (h + 1) * nb], gk)
        vn_ref[0, h] = p[:, OFF_CV + h * nb:OFF_CV + (h + 1) * nb]
        zc_ref[0, h] = p[:, OFF_CZ + h * nb:OFF_CZ + (h + 1) * nb]


def _sample_mix(proj3, ab3, cbuf, conv_w, state, w00, b0, alog, dtb, ggain, gq2, gk2):
    bn = proj3.shape[0]
    nb = HEAD_DIM
    row = lambda w: pl.BlockSpec((1, 1, w), lambda b: (b, 0, 0))
    full2 = lambda a: pl.BlockSpec(a.shape, lambda b: (0, 0))
    st = pl.BlockSpec((1, B_HEADS, nb, nb), lambda b: (b, 0, 0, 0))
    heads = pl.BlockSpec((1, C_HEADS, 1, nb), lambda b: (b, 0, 0, 0))
    sd = lambda w: jax.ShapeDtypeStruct((bn, 1, w), F32)
    hd = jax.ShapeDtypeStruct((bn, C_HEADS, 1, nb), F32)
    return pl.pallas_call(
        _sample_mix_kernel,
        grid=(bn,),
        in_specs=[row(MAIN_COLS), row(GATE_PAD),
                  pl.BlockSpec((1, CONV_WIDTH - 1, 3 * B_W), lambda b: (b, 0, 0)),
                  full2(conv_w), st, full2(w00), full2(b0), full2(alog), full2(dtb),
                  full2(ggain), full2(gq2), full2(gk2)],
        out_specs=[row(A_W), row(A_W), row(B_W), st, heads, heads, heads, heads],
        out_shape=[sd(A_W), sd(A_W), sd(B_W),
                   jax.ShapeDtypeStruct(state.shape, F32), hd, hd, hd, hd],
        compiler_params=_cparams(1),
        name="sample_mix",
    )(proj3, ab3, cbuf, conv_w, state, w00, b0, alog, dtb, ggain, gq2, gk2)


def _paged_attn_kernel(pt_ref, qn_ref, kn_ref, vn_ref, z_ref, slope_ref, lamv_ref, gain_ref, *rest,
                       lam_init, past_len, n_groups):
    del pt_ref
    n_scratch = 6
    pp = (len(rest) - 1 - n_scratch) // 2
    k_refs = rest[:pp]
    v_refs = rest[pp:2 * pp]
    o_ref = rest[2 * pp]
    sc_ref, mrun_ref, lrun_ref, acc_ref, q3_ref, snew_ref = rest[2 * pp + 1:]
    ph = pl.program_id(1)
    g = pl.program_id(2)
    page = k_refs[0].shape[1]
    sshape = mrun_ref.shape

    @pl.when((ph == 0) & (g == 0))
    def _():
        q = jnp.broadcast_to(qn_ref[0], q3_ref.shape)
        ri = lax.broadcasted_iota(jnp.int32, q3_ref.shape, 1)
        ci = lax.broadcasted_iota(jnp.int32, q3_ref.shape, 2)
        q3_ref[...] = jnp.where((ci // QK_DIM) == ri, q, 0.0)
        mrun_ref[...] = jnp.full(sshape, NEG_BIG, F32)
        lrun_ref[...] = jnp.zeros(sshape, F32)
        acc_ref[...] = jnp.zeros(acc_ref.shape, F32)

    @pl.when(ph == 0)
    def _():
        q3 = q3_ref[...].astype(BF16)
        slope = slope_ref[...]
        tok = lax.broadcasted_iota(jnp.int32, sshape, 2)
        mrun = mrun_ref[...]
        for i in range(pp):
            pg = g * pp + i
            s = _bdot_nt(q3, k_refs[i][...].astype(BF16))
            dist = (past_len - (pg * page + tok)).astype(F32)
            s = s - slope * dist
            sc_ref[pg] = s
            mrun = jnp.maximum(mrun, s)
        mrun_ref[...] = mrun

    @pl.when((ph == 1) & (g == 0))
    def _():
        s_new = jnp.sum(q3_ref[...].astype(BF16).astype(F32) * kn_ref[0], axis=-1, keepdims=True)
        m = jnp.maximum(jnp.max(mrun_ref[...], axis=-1, keepdims=True), s_new)
        mrun_ref[...] = jnp.broadcast_to(m, sshape)
        snew_ref[...] = jnp.broadcast_to(s_new, sshape)

    @pl.when(ph == 1)
    def _():
        m = mrun_ref[...]
        lrun = lrun_ref[...]
        acc = acc_ref[...]
        for i in range(pp):
            pg = g * pp + i
            p = jnp.exp(sc_ref[pg] - m)
            lrun = lrun + p
            acc = acc + _bdot(p.astype(BF16), v_refs[i][...].astype(BF16))
        lrun_ref[...] = lrun
        acc_ref[...] = acc

    @pl.when((ph == 1) & (g == n_groups - 1))
    def _():
        m = mrun_ref[:, :, 0:1]
        p_new = jnp.exp(snew_ref[:, :, 0:1] - m)
        l = jnp.sum(lrun_ref[...], axis=-1, keepdims=True) + p_new
        normed = (acc_ref[...] + p_new * vn_ref[0]) / l
        lam = _lambda_value(lamv_ref[...], lam_init)
        o = normed[:, 0:1, :] - lam * normed[:, 1:2, :]
        o = o * lax.rsqrt(jnp.mean(o * o, axis=-1, keepdims=True) + EPS) * gain_ref[...] * (1.0 - lam_init)
        o_ref[0] = o * _silu(z_ref[0])


def _paged_attn(page_table, qn, kn, vn, zc, cache_k, cache_v, layer, slope3, lamv, gain, lam_init):
    bn, n_pages = page_table.shape
    page = cache_k.shape[3]
    pp = math.gcd(PAGES_PER_STEP, n_pages)
    n_groups = n_pages // pp
    past_len = n_pages * page
    head_row = pl.BlockSpec((1, C_HEADS, 1, HEAD_DIM), lambda b, ph, g, pt: (b, 0, 0, 0))

    def k_spec(i):
        def imap(b, ph, g, pt):
            grp = jnp.where(ph == 0, g, n_groups - 1)
            return (layer, pt[b, grp * pp + i], 0, 0, 0)
        return pl.BlockSpec((None, None, C_HEADS, page, HEAD_DIM), imap)

    def v_spec(i):
        def imap(b, ph, g, pt):
            grp = jnp.where(ph == 0, 0, g)
            return (layer, pt[b, grp * pp + i], 0, 0, 0)
        return pl.BlockSpec((None, None, C_HEADS, page, HEAD_DIM), imap)

    def full(a):
        nd = a.ndim
        return pl.BlockSpec(a.shape, lambda b, ph, g, pt: (0,) * nd)

    sshape = (C_HEADS, SCORE_ROWS, page)
    qshape = (C_HEADS, SCORE_ROWS, HEAD_DIM)
    kern = functools.partial(_paged_attn_kernel, lam_init=lam_init, past_len=past_len, n_groups=n_groups)
    grid_spec = pltpu.PrefetchScalarGridSpec(
        num_scalar_prefetch=1,
        grid=(bn, 2, n_groups),
        in_specs=[head_row, head_row, head_row, head_row, full(slope3), full(lamv), full(gain)]
                 + [k_spec(i) for i in range(pp)] + [v_spec(i) for i in range(pp)],
        out_specs=head_row,
        scratch_shapes=[pltpu.VMEM((n_pages,) + sshape, F32),
                        pltpu.VMEM(sshape, F32),
                        pltpu.VMEM(sshape, F32),
                        pltpu.VMEM(qshape, F32),
                        pltpu.VMEM(qshape, F32),
                        pltpu.VMEM(sshape, F32)])
    return pl.pallas_call(
        kern,
        grid_spec=grid_spec,
        out_shape=jax.ShapeDtypeStruct((bn, C_HEADS, 1, HEAD_DIM), F32),
        compiler_params=_cparams(3),
        name="paged_attn",
    )(page_table, qn, kn, vn, zc, slope3, lamv, gain, *([cache_k] * pp), *([cache_v] * pp))


def _pad_lanes(v, width=GATE_PAD):
    return jnp.pad(v.astype(F32), (0, width - v.shape[0])).reshape(1, width)


def _layer_params(l, norm_gain, chunk_w, chunk_b, gdn_conv_w, gdn_a_log, gdn_dt_bias,
                  gdn_norm_gain, attn_q_norm, attn_k_norm, lq1, lk1, lq2, lk2, subln):
    return dict(
        layer=l, norm_gain=norm_gain[l], chunk_w=chunk_w[l],
        bias_b=jnp.broadcast_to(chunk_b[l][:, :, None], (A_GROUPS, CHUNK_MLP, HEAD_DIM)),
        w00=jnp.repeat(chunk_w[l][:, 0, 0], HEAD_DIM).reshape(1, A_W),
        b0=jnp.repeat(chunk_b[l][:, 0], HEAD_DIM).reshape(1, A_W),
        conv_w=gdn_conv_w[l], alog=_pad_lanes(gdn_a_log[l]), dtb=_pad_lanes(gdn_dt_bias[l]),
        ggain=gdn_norm_gain[l].reshape(1, HEAD_DIM),
        gq2=jnp.tile(attn_q_norm[l], 2).reshape(1, HEAD_DIM),
        gk2=jnp.tile(attn_k_norm[l], 2).reshape(1, HEAD_DIM),
        lamv=jnp.stack([lq1[l], lk1[l], lq2[l], lk2[l]]).astype(F32),
        subln=subln[l].reshape(1, HEAD_DIM),
        lam_init=0.8 - 0.6 * math.exp(-0.3 * l),
    )


MAIN_TN = 512
MAIN_TM = 1024
GATING_ROWS = 2048


def _layer(xp, xs, lp, w_in_t, w_out, qaug, kaug, depth, kv_all, slope3, state, cbuf, cache_k, cache_v,
           page_table):
    bn, t, d = xp.shape
    sn = xs.shape[0]
    m = bn * t
    xp2 = xp.reshape(m, d)
    xs2 = xs.reshape(sn, d)
    tm = min(MAIN_TM, m)
    layer = lp['layer']
    h = _rmsnorm(xp2, lp['norm_gain'], min(512, m))
    proj, proj_s = _inproj(h, xs2, lp['norm_gain'], w_in_t, layer,
                           functools.partial(_main_row_start, tn=MAIN_TN), MAIN_COLS, tm, MAIN_TN)
    gates, gates_s = _inproj(h, xs2, lp['norm_gain'], w_in_t, layer, lambda j: GATE_SRC, GATE_PAD, tm, GATE_PAD)

    ps3 = proj_s.reshape(sn, 1, MAIN_COLS)
    a_s, a_v, b_s, s_new, qn, kn_s, vn, zc = _sample_mix(
        ps3, gates_s.reshape(sn, 1, GATE_PAD), cbuf, lp['conv_w'], state, lp['w00'], lp['b0'],
        lp['alog'], lp['dtb'], lp['ggain'], lp['gq2'], lp['gk2'])
    c_s = _paged_attn(page_table, qn, kn_s, vn, zc, cache_k, cache_v, layer, slope3,
                      lp['lamv'], lp['subln'], lp['lam_init'])
    conv_s = jnp.concatenate([cbuf[:, 1:], ps3[:, :, OFF_BQ:OFF_BQ + 3 * B_W]], axis=1)

    proj3 = proj.reshape(bn, t, MAIN_COLS)
    a_out = _gating_prompt(proj, lp['chunk_w'], lp['bias_b'], min(GATING_ROWS, t))
    b_out, s_fin = _gdn_prompt(proj3, gates.reshape(bn, t, GATE_PAD), lp['conv_w'], lp['alog'], lp['dtb'],
                               lp['ggain'])
    q1, q2, kn, k1, k2, vo, vb = _cprep_prompt(proj3, lp['gq2'], lp['gk2'], qaug, kaug, min(256, t),
                                               layer, depth, kv_all)
    c_out = _attn_prompt(q1, q2, k1, k2, vb, proj3, lp['lamv'], lp['subln'], lp['lam_init'],
                         min(ATTN_BLOCK, t))
    conv_p = proj3[:, t - (CONV_WIDTH - 1):, OFF_BQ:OFF_BQ + 3 * B_W]

    yp, ys = _outproj(xp2, a_out, b_out.reshape(m, B_W), c_out.reshape(m, C_W),
                      xs2, a_s.reshape(sn, A_W), b_s.reshape(sn, B_W), c_s.reshape(sn, C_W),
                      w_out, layer, tm, MAIN_TN)
    prompt_out = (yp.reshape(bn, t, d), (kn, vo), s_fin, conv_p)
    sample_out = (ys.reshape(sn, 1, d), kn_s.reshape(sn, 1, C_HEADS, HEAD_DIM), vn.reshape(sn, 1, C_HEADS, HEAD_DIM),
                  s_new, conv_s, a_v)
    return prompt_out, sample_out


def kernel(x_prompt, x_sample, cache_attn_k, cache_attn_v, state_gdn, state_gdn_conv, page_table,
           norm_gain, w_in, w_out, chunk_w, chunk_b, gdn_conv_w, gdn_a_log, gdn_dt_bias, gdn_norm_gain,
           attn_q_norm, attn_k_norm, lambda_q1, lambda_k1, lambda_q2, lambda_k2, attn_subln_gain):
    depth = w_in.shape[0]
    slopes_np = _alibi_slopes(C_HEADS)
    page = cache_attn_k.shape[2]
    slope3 = jnp.asarray(np.broadcast_to(slopes_np[:, None, None], (C_HEADS, SCORE_ROWS, page)).copy())
    qaug, kaug = _alibi_tables(slopes_np, x_prompt.shape[1])
    w_in_t = jnp.transpose(w_in, (0, 2, 1))
    ck = jnp.transpose(cache_attn_k, (0, 1, 3, 2, 4))
    cv = jnp.transpose(cache_attn_v, (0, 1, 3, 2, 4))

    yp, ys = x_prompt, x_sample
    outs = [[] for _ in range(7)]
    kv_all = None
    for l in range(depth):
        lp = _layer_params(l, norm_gain, chunk_w, chunk_b, gdn_conv_w, gdn_a_log, gdn_dt_bias,
                           gdn_norm_gain, attn_q_norm, attn_k_norm, lambda_q1, lambda_k1, lambda_q2,
                           lambda_k2, attn_subln_gain)
        (yp, kv_all, ps, pc), (ys, sk, sv, ss, sc, sa) = _layer(
            yp, ys, lp, w_in_t, w_out, qaug, kaug, depth, kv_all, slope3, state_gdn[l], state_gdn_conv[l],
            ck, cv, page_table)
        for lst, val in zip(outs, (ps, pc, sk, sv, ss, sc, sa)):
            lst.append(val)
    res = [jnp.stack(o) for o in outs]
    pk = jnp.transpose(kv_all[0], (0, 1, 3, 2, 4))
    pv = jnp.transpose(kv_all[1], (0, 1, 3, 2, 4))
    return (yp, ys, pk, pv) + tuple(res)
```

```python
import functools
import math

import jax
import jax.numpy as jnp
import numpy as np
from jax import lax
from jax.experimental import pallas as pl
from jax.experimental.pallas import tpu as pltpu

F32 = jnp.float32
BF16 = jnp.bfloat16
HIGHEST = lax.Precision.HIGHEST

HEAD_DIM = 128
A_GROUPS = 8
B_HEADS = 12
C_HEADS = 12
A_W = A_GROUPS * HEAD_DIM
B_W = B_HEADS * HEAD_DIM
C_W = C_HEADS * HEAD_DIM
QK_DIM = HEAD_DIM // 2
CHUNK_MLP = 128
GDN_CHUNK = 64
CONV_WIDTH = 4
EPS = 1e-6
INV_SQRT2 = 0.7071067811865476
NEG_BIG = -1e30

OFF_AU, OFF_AV, OFF_AZ = 0, A_W, 2 * A_W
OFF_BQ = 3 * A_W
OFF_BK = OFF_BQ + B_W
OFF_BV = OFF_BK + B_W
OFF_BZ = OFF_BV + B_W
OFF_CQ = OFF_BZ + B_W
OFF_CK = OFF_CQ + C_W
OFF_CV = OFF_CK + C_W
OFF_CZ = OFF_CV + C_W
MAIN_COLS = OFF_CZ + C_W
GATE_SRC = 3 * A_W + 4 * B_W
GATE_PAD = 128

VMEM_LIMIT = 56 * 1024 * 1024
PAGES_PER_STEP = 8
SCORE_ROWS = 8


def _cparams(n_axes):
    return pltpu.CompilerParams(dimension_semantics=("arbitrary",) * n_axes,
                                vmem_limit_bytes=VMEM_LIMIT)


def _gelu(x):
    return 0.5 * x * (1.0 + lax.erf(x * INV_SQRT2))


def _silu(x):
    return x * jax.nn.sigmoid(x)


def _dot(a, b):
    return jnp.dot(a, b, preferred_element_type=F32)


def _dot_hi(a, b):
    return jnp.dot(a, b, preferred_element_type=F32, precision=HIGHEST)


def _dot_nt(a, b):
    return lax.dot_general(a, b, (((1,), (1,)), ((), ())), preferred_element_type=F32)


def _dot_tn(a, b, precision=None):
    return lax.dot_general(a, b, (((0,), (0,)), ((), ())), preferred_element_type=F32,
                           precision=precision)


def _bdot_nt(a, b):
    return lax.dot_general(a, b, (((2,), (2,)), ((0,), (0,))), preferred_element_type=F32)


def _bdot(a, b):
    return lax.dot_general(a, b, (((2,), (1,)), ((0,), (0,))), preferred_element_type=F32)


def _alibi_slopes(n):
    def pow2(m):
        start = 2.0 ** (-8.0 / m)
        return [start ** (i + 1) for i in range(m)]
    p = 2 ** int(math.floor(math.log2(n)))
    s = pow2(p)
    if p < n:
        s = s + pow2(2 * p)[0::2][: n - p]
    return np.array(s, dtype=np.float32)


def _rms_bf16(x, gain):
    ms = jnp.mean(x * x, axis=-1, keepdims=True)
    return (x * lax.rsqrt(ms + EPS) * gain).astype(BF16)


def _norm_gates_kernel(x_ref, g_ref, w_ref, xs_ref, h_ref, o_ref, os_ref, wt_ref):
    @pl.when(pl.program_id(0) == 0)
    def _():
        wt_ref[...] = w_ref[0].astype(BF16).T
        os_ref[...] = _dot(_rms_bf16(xs_ref[...], g_ref[...]), wt_ref[...])
    h = _rms_bf16(x_ref[...], g_ref[...])
    h_ref[...] = h
    o_ref[...] = _dot(h, wt_ref[...])


def _norm_gates(x, xs, gain, w_t, layer, tm):
    m, d = x.shape
    ms = xs.shape[0]
    return pl.pallas_call(
        _norm_gates_kernel,
        grid=(m // tm,),
        in_specs=[pl.BlockSpec((tm, d), lambda i: (i, 0)),
                  pl.BlockSpec((1, d), lambda i: (0, 0)),
                  pl.BlockSpec((pl.Element(1), pl.Element(GATE_PAD), pl.Element(d)),
                               lambda i: (layer, GATE_SRC, 0)),
                  pl.BlockSpec((ms, d), lambda i: (0, 0))],
        out_specs=[pl.BlockSpec((tm, d), lambda i: (i, 0)),
                   pl.BlockSpec((tm, GATE_PAD), lambda i: (i, 0)),
                   pl.BlockSpec((ms, GATE_PAD), lambda i: (0, 0))],
        out_shape=[jax.ShapeDtypeStruct((m, d), BF16),
                   jax.ShapeDtypeStruct((m, GATE_PAD), F32),
                   jax.ShapeDtypeStruct((ms, GATE_PAD), F32)],
        scratch_shapes=[pltpu.VMEM((d, GATE_PAD), BF16)],
        compiler_params=_cparams(1),
        name="norm_gates",
    )(x, gain.reshape(1, d), w_t, xs)


def _main_row_start(j, tn):
    return pl.multiple_of(j * tn + jnp.where(j * tn >= GATE_SRC, 2 * B_HEADS, 0), 8)


def _wt_spec(layer, tn, k, row_start):
    return pl.BlockSpec((pl.Element(1), pl.Element(tn), pl.Element(k)),
                        lambda j, i: (layer, row_start(j), 0))


def _inproj_kernel(a_ref, w_ref, xs_ref, g_ref, o_ref, os_ref, wt_ref):
    @pl.when(pl.program_id(1) == 0)
    def _():
        wt_ref[...] = w_ref[0].astype(BF16).T
        os_ref[...] = _dot(_rms_bf16(xs_ref[...], g_ref[...]), wt_ref[...])
    o_ref[...] = _dot(a_ref[...], wt_ref[...])


def _inproj(a, xs, gain, w_t, layer, row_start, n_out, tm, tn):
    m, k = a.shape
    ms = xs.shape[0]
    return pl.pallas_call(
        _inproj_kernel,
        grid=(n_out // tn, m // tm),
        in_specs=[pl.BlockSpec((tm, k), lambda j, i: (i, 0)),
                  _wt_spec(layer, tn, k, row_start),
                  pl.BlockSpec((ms, k), lambda j, i: (0, 0)),
                  pl.BlockSpec((1, k), lambda j, i: (0, 0))],
        out_specs=[pl.BlockSpec((tm, tn), lambda j, i: (i, j)),
                   pl.BlockSpec((ms, tn), lambda j, i: (0, j))],
        out_shape=[jax.ShapeDtypeStruct((m, n_out), F32),
                   jax.ShapeDtypeStruct((ms, n_out), F32)],
        scratch_shapes=[pltpu.VMEM((k, tn), BF16)],
        compiler_params=_cparams(2),
        name="in_proj",
    )(a, w_t, xs, gain.reshape(1, k))


def _mix_dot(a_ref, b_ref, c_ref, wb_ref):
    ka = a_ref.shape[1]
    kb = ka + b_ref.shape[1]
    acc = _dot(a_ref[...].astype(BF16), wb_ref[0:ka, :])
    acc = acc + _dot(b_ref[...].astype(BF16), wb_ref[ka:kb, :])
    return acc + _dot(c_ref[...].astype(BF16), wb_ref[kb:, :])


def _outproj_kernel(x_ref, a_ref, b_ref, c_ref, w_ref, xs_ref, as_ref, bs_ref, cs_ref,
                    o_ref, os_ref, wb_ref):
    @pl.when(pl.program_id(1) == 0)
    def _():
        wb_ref[...] = w_ref[...].astype(BF16)
        os_ref[...] = xs_ref[...] + _mix_dot(as_ref, bs_ref, cs_ref, wb_ref)
    o_ref[...] = x_ref[...] + _mix_dot(a_ref, b_ref, c_ref, wb_ref)


def _outproj(x, a, b, c, xs, a_s, b_s, c_s, w_out, layer, tm, tn):
    m, d = x.shape
    ms = xs.shape[0]
    k = w_out.shape[1]
    rows = lambda arr: pl.BlockSpec((tm, arr.shape[1]), lambda j, i: (i, 0))
    whole = lambda arr: pl.BlockSpec(arr.shape, lambda j, i: (0, 0))
    return pl.pallas_call(
        _outproj_kernel,
        grid=(d // tn, m // tm),
        in_specs=[pl.BlockSpec((tm, tn), lambda j, i: (i, j)), rows(a), rows(b), rows(c),
                  pl.BlockSpec((None, k, tn), lambda j, i: (layer, 0, j)),
                  pl.BlockSpec((ms, tn), lambda j, i: (0, j)), whole(a_s), whole(b_s), whole(c_s)],
        out_specs=[pl.BlockSpec((tm, tn), lambda j, i: (i, j)),
                   pl.BlockSpec((ms, tn), lambda j, i: (0, j))],
        out_shape=[jax.ShapeDtypeStruct((m, d), F32), jax.ShapeDtypeStruct((ms, d), F32)],
        scratch_shapes=[pltpu.VMEM((k, tn), BF16)],
        compiler_params=_cparams(2),
        name="out_proj",
    )(x, a, b, c, w_out, xs, a_s, b_s, c_s)


def _gating_kernel(u_ref, v_ref, z_ref, w_ref, b_ref, o_ref):
    rows = u_ref.shape[0]
    ri = lax.broadcasted_iota(jnp.int32, (CHUNK_MLP, CHUNK_MLP), 0)
    ci = lax.broadcasted_iota(jnp.int32, (CHUNK_MLP, CHUNK_MLP), 1)
    w = jnp.where(ri >= ci, w_ref[0], 0.0).astype(BF16)
    bias = b_ref[0]
    for c in range(rows // CHUNK_MLP):
        sl = slice(c * CHUNK_MLP, (c + 1) * CHUNK_MLP)
        vg = _gelu(v_ref[sl, :]).astype(BF16)
        mixed = _dot(w, vg) + bias
        o_ref[sl, :] = (_gelu(u_ref[sl, :]) * mixed * _silu(z_ref[sl, :])).astype(o_ref.dtype)


def _gating_prompt(proj, chunk_w, bias_b, rows):
    m = proj.shape[0]
    nb = HEAD_DIM
    return pl.pallas_call(
        _gating_kernel,
        grid=(A_GROUPS, m // rows),
        in_specs=[pl.BlockSpec((rows, nb), lambda g, i: (i, OFF_AU // nb + g)),
                  pl.BlockSpec((rows, nb), lambda g, i: (i, OFF_AV // nb + g)),
                  pl.BlockSpec((rows, nb), lambda g, i: (i, OFF_AZ // nb + g)),
                  pl.BlockSpec((1, CHUNK_MLP, CHUNK_MLP), lambda g, i: (g, 0, 0)),
                  pl.BlockSpec((1, CHUNK_MLP, nb), lambda g, i: (g, 0, 0))],
        out_specs=pl.BlockSpec((rows, nb), lambda g, i: (i, g)),
        out_shape=jax.ShapeDtypeStruct((m, A_W), BF16),
        compiler_params=_cparams(2),
        name="gating_prompt",
    )(proj, proj, proj, chunk_w, bias_b)


GDN_HEADS_PER_STEP = 12
GDN_ROWS_PER_STEP = 512
NEUMANN_SPLIT_STEPS = 2


def _split2(a):
    hi = a.astype(BF16)
    lo = (a - hi.astype(F32)).astype(BF16)
    return hi, lo


def _dot3(ah, al, bh, bl):
    return _dot(ah, bh) + (_dot(ah, bl) + _dot(al, bh))


def _gdn_prompt_kernel(q_ref, k_ref, v_ref, z_ref, ab_ref, wq_ref, wk_ref, wv_ref,
                       alog_ref, dtb_ref, gain_ref, o_ref, s_ref, st_ref, tail_ref):
    hg = pl.program_id(1)
    tb = pl.program_id(2)
    rows = q_ref.shape[1]
    heads = GDN_HEADS_PER_STEP
    pair = 2 * GDN_CHUNK

    ri = lax.broadcasted_iota(jnp.int32, (pair, pair), 0)
    ci = lax.broadcasted_iota(jnp.int32, (pair, pair), 1)
    same = (ri >= GDN_CHUNK) == (ci >= GDN_CHUNK)
    incl = same & (ri >= ci)
    strict = same & (ri > ci)
    incl16 = jnp.where(incl, 1.0, 0.0).astype(BF16)
    eye = jnp.where(ri == ci, 1.0, 0.0)
    row_lo = lax.broadcasted_iota(jnp.int32, (pair, HEAD_DIM), 0) < GDN_CHUNK
    gain = gain_ref[...]

    @pl.when(tb == 0)
    def _():
        st_ref[...] = jnp.zeros_like(st_ref)
        tail_ref[...] = jnp.zeros_like(tail_ref)

    hs = range(heads)
    cols = [slice(g * HEAD_DIM, (g + 1) * HEAD_DIM) for g in hs]
    pick_a = [jnp.where(ri == (hg * heads + g), 1.0, 0.0).astype(BF16) for g in hs]
    pick_b = [jnp.where(ri == (B_HEADS + hg * heads + g), 1.0, 0.0).astype(BF16) for g in hs]
    neg_a_row = -jnp.exp(alog_ref[...])
    dtb_row = dtb_ref[...]

    def split3(x):
        hi = x.astype(BF16)
        r = x - hi.astype(F32)
        mid = r.astype(BF16)
        return hi, mid, (r - mid.astype(F32)).astype(BF16)

    def body(i, carry):
        r0 = pl.multiple_of(i * pair, pair)
        rp = pl.multiple_of(jnp.maximum(r0 - 8, 0), 8)
        ab = ab_ref[0, pl.ds(r0, pair), :]

        def conv(x_ref, w_ref, part, g):
            cur = x_ref[0, pl.ds(r0, pair), cols[g]]
            prev = jnp.where(i > 0, x_ref[0, pl.ds(rp, 8), cols[g]], tail_ref[part, :, cols[g]])
            w = w_ref[:, cols[g]]
            head = jnp.concatenate([prev, cur[0:8]], axis=0)
            acc = cur * w[CONV_WIDTH - 1:CONV_WIDTH]
            for s in range(1, CONV_WIDTH):
                shifted = jnp.concatenate([head[8 - s:16 - s], pltpu.roll(cur, s, 0)[8:]], axis=0)
                acc = acc + shifted * w[CONV_WIDTH - 1 - s:CONV_WIDTH - s]
            return _silu(acc)

        def l2n(x):
            return x * lax.rsqrt(jnp.sum(x * x, axis=-1, keepdims=True) + EPS)

        q = [l2n(conv(q_ref, wq_ref, 0, g)) * (HEAD_DIM ** -0.5) for g in hs]
        k = [l2n(conv(k_ref, wk_ref, 1, g)) for g in hs]
        v = [conv(v_ref, wv_ref, 2, g) for g in hs]
        g_all = neg_a_row * jax.nn.softplus(ab + dtb_row)
        gp = split3(g_all)
        gc_all = _dot(incl16, gp[0]) + (_dot(incl16, gp[1]) + _dot(incl16, gp[2]))
        cp = split3(gc_all)
        bp = _split2(jax.nn.sigmoid(ab))
        gcb = [_dot(cp[0], pick_a[g]) + (_dot(cp[1], pick_a[g]) + _dot(cp[2], pick_a[g])) for g in hs]
        beta = [_dot(bp[0], pick_b[g]) + _dot(bp[1], pick_b[g]) for g in hs]
        egc = [jnp.exp(x) for x in gcb]
        glast = [jnp.where(row_lo, x[GDN_CHUNK - 1:GDN_CHUNK, :], x[pair - 1:pair, :]) for x in gcb]
        kdec = [k[g] * jnp.exp(glast[g] - gcb[g]) for g in hs]
        decay = [jnp.where(incl, jnp.exp(jnp.where(incl, x - x.T, 0.0)), 0.0) for x in gcb]

        kb = [k[g] * beta[g] for g in hs]
        k16 = [x.astype(BF16) for x in k]
        lmat = [jnp.where(strict, _dot_nt(kb[g].astype(BF16), k16[g]) * decay[g], 0.0) for g in hs]
        tinv = [eye - x for x in lmat]
        lsp = [_split2(x) for x in lmat]
        pw = [_dot3(lh, ll, lh, ll) for lh, ll in lsp]
        for step in range(5):
            if step < NEUMANN_SPLIT_STEPS:
                psp = [_split2(x) for x in pw]
                tsp = [_split2(x) for x in tinv]
                tinv = [tinv[g] + _dot3(tsp[g][0], tsp[g][1], psp[g][0], psp[g][1]) for g in hs]
                pw = [_dot3(ph, pl_, ph, pl_) for ph, pl_ in psp]
            else:
                p16 = [x.astype(BF16) for x in pw]
                tinv = [tinv[g] + _dot(tinv[g].astype(BF16), p16[g]) for g in hs]
                if step < 4:
                    pw = [_dot(x, x) for x in p16]
        t16 = [x.astype(BF16) for x in tinv]
        u = [_dot(t16[g], (v[g] * beta[g]).astype(BF16)) for g in hs]
        w16 = [_dot(t16[g], (kb[g] * egc[g]).astype(BF16)).astype(BF16) for g in hs]
        a_intra = [(_dot_nt(q[g].astype(BF16), k16[g]) * decay[g]).astype(BF16) for g in hs]
        qdec = [(q[g] * egc[g]).astype(BF16) for g in hs]

        s = [st_ref[g] for g in hs]
        outs = [[] for _ in hs]
        for c in range(2):
            sl = slice(c * GDN_CHUNK, (c + 1) * GDN_CHUNK)
            in_chunk = row_lo if c == 0 else jnp.logical_not(row_lo)
            s16 = [x.astype(BF16) for x in s]
            vn16 = [jnp.where(in_chunk, u[g] - _dot(w16[g], s16[g]), 0.0).astype(BF16) for g in hs]
            for g in hs:
                outs[g].append(_dot(qdec[g][sl], s16[g]) + _dot(a_intra[g][sl], vn16[g]))
            gt = [jnp.exp(x[(c + 1) * GDN_CHUNK - 1:(c + 1) * GDN_CHUNK, :]) for x in gcb]
            kd = [jnp.where(in_chunk, x, 0.0).astype(BF16) for x in kdec]
            s = [s[g] * gt[g] + _dot_tn(kd[g], vn16[g]) for g in hs]
        for g in hs:
            st_ref[g] = s[g]
            o = jnp.concatenate(outs[g], axis=0)
            o = o * lax.rsqrt(jnp.mean(o * o, axis=-1, keepdims=True) + EPS) * gain
            o_ref[0, pl.ds(r0, pair), cols[g]] = (
                o * _silu(z_ref[0, pl.ds(r0, pair), cols[g]])).astype(o_ref.dtype)
        return carry

    lax.fori_loop(0, rows // pair, body, 0)
    for part, x_ref in enumerate((q_ref, k_ref, v_ref)):
        tail_ref[part] = x_ref[0, rows - 8:rows, :]

    @pl.when(tb == pl.num_programs(2) - 1)
    def _():
        s_ref[0] = st_ref[...]


def _gdn_prompt(proj3, ab3, conv_w, alog, dtb, gain):
    bn, t, _ = proj3.shape
    heads = GDN_HEADS_PER_STEP
    rows = min(GDN_ROWS_PER_STEP, t)
    width = heads * HEAD_DIM
    n_hg = B_HEADS // heads
    col = lambda off: (lambda b, hg, tb: (b, tb, off // width + hg))
    wcol = lambda part: (lambda b, hg, tb: (0, part * n_hg + hg))
    vec = pl.BlockSpec((1, HEAD_DIM), lambda b, hg, tb: (0, 0))
    return pl.pallas_call(
        _gdn_prompt_kernel,
        grid=(bn, n_hg, t // rows),
        in_specs=[pl.BlockSpec((1, rows, width), col(OFF_BQ)),
                  pl.BlockSpec((1, rows, width), col(OFF_BK)),
                  pl.BlockSpec((1, rows, width), col(OFF_BV)),
                  pl.BlockSpec((1, rows, width), col(OFF_BZ)),
                  pl.BlockSpec((1, rows, GATE_PAD), lambda b, hg, tb: (b, tb, 0)),
                  pl.BlockSpec((CONV_WIDTH, width), wcol(0)),
                  pl.BlockSpec((CONV_WIDTH, width), wcol(1)),
                  pl.BlockSpec((CONV_WIDTH, width), wcol(2)),
                  vec, vec, vec],
        out_specs=[pl.BlockSpec((1, rows, width), lambda b, hg, tb: (b, tb, hg)),
                   pl.BlockSpec((1, heads, HEAD_DIM, HEAD_DIM), lambda b, hg, tb: (b, hg, 0, 0))],
        out_shape=[jax.ShapeDtypeStruct((bn, t, B_W), BF16),
                   jax.ShapeDtypeStruct((bn, B_HEADS, HEAD_DIM, HEAD_DIM), F32)],
        scratch_shapes=[pltpu.VMEM((heads, HEAD_DIM, HEAD_DIM), F32),
                        pltpu.VMEM((3, 8, width), F32)],
        compiler_params=_cparams(3),
        name="gdn_prompt",
    )(proj3, proj3, proj3, proj3, ab3, conv_w, conv_w, conv_w, alog, dtb, gain)


AUG_LANE0 = QK_DIM
POS_SPLIT = 64
ATTN_BLOCK = 512
ATTN_HEADS_PER_STEP = 1


def _half_rmsnorm(x, gain2):
    lo = lax.broadcasted_iota(jnp.int32, x.shape, x.ndim - 1) < QK_DIM
    x2 = x * x
    s_lo = jnp.sum(jnp.where(lo, x2, 0.0), axis=-1, keepdims=True)
    s_hi = jnp.sum(jnp.where(lo, 0.0, x2), axis=-1, keepdims=True)
    ms = jnp.where(lo, s_lo, s_hi) * (1.0 / QK_DIM)
    return x * lax.rsqrt(ms + EPS) * gain2


def _alibi_tables(slopes_np, t):
    import ml_dtypes
    bf = ml_dtypes.bfloat16
    hi = slopes_np.astype(bf).astype(np.float32)
    mid = (slopes_np - hi).astype(bf).astype(np.float32)
    lo = (slopes_np - hi - mid).astype(bf).astype(np.float32)
    qaug = np.zeros((C_HEADS, HEAD_DIM), np.float32)
    for n, piece in enumerate((hi, mid, lo)):
        qaug[:, AUG_LANE0 + n] = POS_SPLIT * piece
        qaug[:, AUG_LANE0 + 3 + n] = piece
    pos = np.arange(t)
    kaug = np.zeros((t, HEAD_DIM), np.float32)
    kaug[:, AUG_LANE0:AUG_LANE0 + 3] = (pos // POS_SPLIT)[:, None]
    kaug[:, AUG_LANE0 + 3:AUG_LANE0 + 6] = (pos % POS_SPLIT)[:, None]
    return jnp.asarray(qaug.reshape(1, C_W)), jnp.asarray(kaug)


def _cprep_kernel(q_ref, k_ref, v_ref, gq_ref, gk_ref, qaug_ref, kaug_ref, *rest):
    q1_ref, q2_ref, kn_ref, k1_ref, k2_ref, vo_ref, vb_ref = rest[-7:]
    gq = gq_ref[...]
    gk = gk_ref[...]
    kaug = kaug_ref[...]
    lo = lax.broadcasted_iota(jnp.int32, kaug.shape, 1) < QK_DIM
    for h in range(C_HEADS):
        sl = slice(h * HEAD_DIM, (h + 1) * HEAD_DIM)
        qn = _half_rmsnorm(q_ref[0, :, sl], gq) * (QK_DIM ** -0.5)
        kn = _half_rmsnorm(k_ref[0, :, sl], gk)
        qaug = qaug_ref[:, sl]
        q1_ref[0, :, sl] = jnp.where(lo, qn, qaug).astype(BF16)
        q2_ref[0, :, sl] = jnp.where(lo, pltpu.roll(qn, QK_DIM, 1), qaug).astype(BF16)
        kn_ref[0, h] = kn
        k1_ref[0, :, sl] = jnp.where(lo, kn, kaug).astype(BF16)
        k2_ref[0, :, sl] = jnp.where(lo, pltpu.roll(kn, QK_DIM, 1), kaug).astype(BF16)
        vo_ref[0, h] = v_ref[0, :, sl]
    vb_ref[...] = v_ref[...].astype(BF16)


def _cprep_prompt(proj3, gq2, gk2, qaug, kaug, tr, layer, depth, kv_all):
    bn, t, _ = proj3.shape
    blk = lambda off: pl.BlockSpec((1, tr, C_W), lambda b, r: (b, r, off // C_W))
    out = pl.BlockSpec((1, tr, C_W), lambda b, r: (b, r, 0))
    out_hm = pl.BlockSpec((None, 1, C_HEADS, tr, HEAD_DIM), lambda b, r: (layer, b, 0, r, 0))
    vec = pl.BlockSpec((1, HEAD_DIM), lambda b, r: (0, 0))
    sd = lambda dt: jax.ShapeDtypeStruct((bn, t, C_W), dt)
    sd_hm = jax.ShapeDtypeStruct((depth, bn, C_HEADS, t, HEAD_DIM), F32)
    in_specs = [blk(OFF_CQ), blk(OFF_CK), blk(OFF_CV), vec, vec,
                pl.BlockSpec((1, C_W), lambda b, r: (0, 0)),
                pl.BlockSpec((tr, HEAD_DIM), lambda b, r: (r, 0))]
    args = [proj3, proj3, proj3, gq2, gk2, qaug, kaug]
    aliases = {}
    if kv_all is not None:
        in_specs += [pl.BlockSpec(memory_space=pl.ANY)] * 2
        aliases = {len(args): 2, len(args) + 1: 5}
        args += list(kv_all)
    return pl.pallas_call(
        _cprep_kernel,
        grid=(bn, t // tr),
        in_specs=in_specs,
        out_specs=[out, out, out_hm, out, out, out_hm, out],
        out_shape=[sd(BF16), sd(BF16), sd_hm, sd(BF16), sd(BF16), sd_hm, sd(BF16)],
        input_output_aliases=aliases,
        compiler_params=_cparams(2),
        name="cprep_prompt",
    )(*args)


def _lambda_value(lamv, lam_init):
    e1 = jnp.exp(jnp.sum(lamv[0:1] * lamv[1:2], axis=-1, keepdims=True))
    e2 = jnp.exp(jnp.sum(lamv[2:3] * lamv[3:4], axis=-1, keepdims=True))
    return e1 - e2 + lam_init


def _attn_prompt_kernel(q1_ref, q2_ref, k1_ref, k2_ref, v_ref, z_ref, lamv_ref, gain_ref, o_ref,
                        *, lam_init, tq):
    qi = pl.program_id(2)
    heads = q1_ref.shape[2] // HEAD_DIM
    cols = [slice(g * HEAD_DIM, (g + 1) * HEAD_DIM) for g in range(heads)]
    chains = [(g, c) for g in range(heads) for c in range(2)]
    nc = len(chains)
    q = [(q1_ref, q2_ref)[c][0, :, cols[g]] for g, c in chains]
    causal = (lax.broadcasted_iota(jnp.int32, (tq, tq), 1) <= lax.broadcasted_iota(jnp.int32, (tq, tq), 0))

    def step(j, carry, masked):
        m, l, acc = carry[0::3], carry[1::3], carry[2::3]
        c0 = pl.multiple_of(j * tq, tq)
        vblk = [v_ref[0, pl.ds(c0, tq), cols[g]] for g in range(heads)]
        s = [_dot_nt(q[n], (k1_ref, k2_ref)[c][0, pl.ds(c0, tq), cols[g]]) for n, (g, c) in enumerate(chains)]
        if masked:
            s = [jnp.where(causal, x, NEG_BIG) for x in s]
        m_new = [jnp.maximum(m[n], jnp.max(s[n], axis=-1, keepdims=True)) for n in range(nc)]
        alpha = [jnp.exp(m[n] - m_new[n]) for n in range(nc)]
        p = [jnp.exp(s[n] - m_new[n]) for n in range(nc)]
        l = [alpha[n] * l[n] + jnp.sum(p[n], axis=-1, keepdims=True) for n in range(nc)]
        pv = [_dot(p[n].astype(BF16), vblk[g]) for n, (g, c) in enumerate(chains)]
        acc = [alpha[n] * acc[n] + pv[n] for n in range(nc)]
        out = []
        for n in range(nc):
            out += [m_new[n], l[n], acc[n]]
        return tuple(out)

    m0 = jnp.full((tq, 1), NEG_BIG, F32)
    l0 = jnp.zeros((tq, 1), F32)
    a0 = jnp.zeros((tq, HEAD_DIM), F32)
    carry = lax.fori_loop(0, qi, lambda j, c: step(j, c, False), (m0, l0, a0) * nc)
    carry = step(qi, carry, True)
    l, acc = carry[1::3], carry[2::3]
    lam = _lambda_value(lamv_ref[...], lam_init)
    for g in range(heads):
        o = acc[2 * g] / l[2 * g] - lam * (acc[2 * g + 1] / l[2 * g + 1])
        o = o * lax.rsqrt(jnp.mean(o * o, axis=-1, keepdims=True) + EPS) * gain_ref[...] * (1.0 - lam_init)
        o_ref[0, :, cols[g]] = (o * _silu(z_ref[0, :, cols[g]])).astype(o_ref.dtype)


def _attn_prompt(q1, q2, k1, k2, vb, proj3, lamv, gain, lam_init, tq):
    bn, t, _ = q1.shape
    nb = ATTN_HEADS_PER_STEP * HEAD_DIM
    kern = functools.partial(_attn_prompt_kernel, lam_init=lam_init, tq=tq)
    qspec = pl.BlockSpec((1, tq, nb), lambda b, h, i: (b, i, h))
    kspec = pl.BlockSpec((1, t, nb), lambda b, h, i: (b, 0, h))
    return pl.pallas_call(
        kern,
        grid=(bn, C_HEADS // ATTN_HEADS_PER_STEP, t // tq),
        in_specs=[qspec, qspec, kspec, kspec, kspec,
                  pl.BlockSpec((1, tq, nb), lambda b, h, i: (b, i, OFF_CZ // nb + h)),
                  pl.BlockSpec((4, QK_DIM), lambda b, h, i: (0, 0)),
                  pl.BlockSpec((1, HEAD_DIM), lambda b, h, i: (0, 0))],
        out_specs=pl.BlockSpec((1, tq, nb), lambda b, h, i: (b, i, h)),
        out_shape=jax.ShapeDtypeStruct((bn, t, C_W), BF16),
        compiler_params=_cparams(3),
        name="attn_prompt",
    )(q1, q2, k1, k2, vb, proj3, lamv, gain)


def _sample_mix_kernel(p_ref, ab_ref, cbuf_ref, cw_ref, st_ref, w00_ref, b0_ref, alog_ref, dtb_ref,
                       ggain_ref, gq_ref, gk_ref,
                       aout_ref, av_ref, bout_ref, snew_ref, qn_ref, kn_ref, vn_ref, zc_ref):
    nb = HEAD_DIM
    p = p_ref[0]
    av = _gelu(p[:, OFF_AV:OFF_AV + A_W])
    av_ref[0] = av
    mixed = w00_ref[...] * av + b0_ref[...]
    aout_ref[0] = _gelu(p[:, OFF_AU:OFF_AU + A_W]) * mixed * _silu(p[:, OFF_AZ:OFF_AZ + A_W])

    cb = cbuf_ref[0]
    cw = cw_ref[...]
    x = p[:, OFF_BQ:OFF_BQ + 3 * B_W]
    acc = cb[0:1] * cw[0:1] + cb[1:2] * cw[1:2] + cb[2:3] * cw[2:3] + x * cw[3:4]
    act = _silu(acc)
    ab = ab_ref[0]
    g_row = -jnp.exp(alog_ref[...]) * jax.nn.softplus(ab + dtb_ref[...])
    beta_row = jax.nn.sigmoid(ab)
    ggain = ggain_ref[...]
    row8 = lax.broadcasted_iota(jnp.int32, (8, nb), 0)
    for h in range(B_HEADS):
        q = act[:, h * nb:(h + 1) * nb]
        k = act[:, B_W + h * nb:B_W + (h + 1) * nb]
        v = act[:, 2 * B_W + h * nb:2 * B_W + (h + 1) * nb]
        q = q * lax.rsqrt(jnp.sum(q * q, axis=-1, keepdims=True) + EPS) * (nb ** -0.5)
        k = k * lax.rsqrt(jnp.sum(k * k, axis=-1, keepdims=True) + EPS)
        eg = jnp.exp(g_row[:, h:h + 1])
        beta = beta_row[:, B_HEADS + h:B_HEADS + h + 1]
        s = st_ref[0, h]
        lhs = jnp.where(row8 == 0, k * (beta * eg), jnp.where(row8 == 1, q * eg, 0.0))
        rs = _dot_hi(lhs, s)
        v_new = v * beta - rs[0:1]
        o = rs[1:2] + jnp.sum(q * k, axis=-1, keepdims=True) * v_new
        k8 = jnp.where(row8 == 0, k, 0.0)
        v8 = jnp.where(row8 == 0, v_new, 0.0)
        snew_ref[0, h] = s * eg + _dot_tn(k8, v8, precision=HIGHEST)
        o = o * lax.rsqrt(jnp.mean(o * o, axis=-1, keepdims=True) + EPS) * ggain
        bout_ref[0, :, h * nb:(h + 1) * nb] = o * _silu(p[:, OFF_BZ + h * nb:OFF_BZ + (h + 1) * nb])

    gq = gq_ref[...]
    gk = gk_ref[...]
    for h in range(C_HEADS):
        qn_ref[0, h] = _half_rmsnorm(p[:, OFF_CQ + h * nb:OFF_CQ + (h + 1) * nb], gq) * (QK_DIM ** -0.5)
        kn_ref[0, h] = _half_rmsnorm(p[:, OFF_CK + h * nb:OFF_CK + (h + 1) * nb], gk)
        vn_ref[0, h] = p[:, OFF_CV + h * nb:OFF_CV + (h + 1) * nb]
        zc_ref[0, h] = p[:, OFF_CZ + h * nb:OFF_CZ + (h + 1) * nb]


def _sample_mix(proj3, ab3, cbuf, conv_w, state, w00, b0, alog, dtb, ggain, gq2, gk2):
    bn = proj3.shape[0]
    nb = HEAD_DIM
    row = lambda w: pl.BlockSpec((1, 1, w), lambda b: (b, 0, 0))
    full2 = lambda a: pl.BlockSpec(a.shape, lambda b: (0, 0))
    st = pl.BlockSpec((1, B_HEADS, nb, nb), lambda b: (b, 0, 0, 0))
    heads = pl.BlockSpec((1, C_HEADS, 1, nb), lambda b: (b, 0, 0, 0))
    sd = lambda w: jax.ShapeDtypeStruct((bn, 1, w), F32)
    hd = jax.ShapeDtypeStruct((bn, C_HEADS, 1, nb), F32)
    return pl.pallas_call(
        _sample_mix_kernel,
        grid=(bn,),
        in_specs=[row(MAIN_COLS), row(GATE_PAD),
                  pl.BlockSpec((1, CONV_WIDTH - 1, 3 * B_W), lambda b: (b, 0, 0)),
                  full2(conv_w), st, full2(w00), full2(b0), full2(alog), full2(dtb),
                  full2(ggain), full2(gq2), full2(gk2)],
        out_specs=[row(A_W), row(A_W), row(B_W), st, heads, heads, heads, heads],
        out_shape=[sd(A_W), sd(A_W), sd(B_W),
                   jax.ShapeDtypeStruct(state.shape, F32), hd, hd, hd, hd],
        compiler_params=_cparams(1),
        name="sample_mix",
    )(proj3, ab3, cbuf, conv_w, state, w00, b0, alog, dtb, ggain, gq2, gk2)


def _paged_attn_kernel(pt_ref, qn_ref, kn_ref, vn_ref, z_ref, slope_ref, lamv_ref, gain_ref, *rest,
                       lam_init, past_len, n_groups):
    del pt_ref
    n_scratch = 6
    pp = (len(rest) - 1 - n_scratch) // 2
    k_refs = rest[:pp]
    v_refs = rest[pp:2 * pp]
    o_ref = rest[2 * pp]
    sc_ref, mrun_ref, lrun_ref, acc_ref, q3_ref, snew_ref = rest[2 * pp + 1:]
    ph = pl.program_id(1)
    g = pl.program_id(2)
    page = k_refs[0].shape[1]
    sshape = mrun_ref.shape

    @pl.when((ph == 0) & (g == 0))
    def _():
        q = jnp.broadcast_to(qn_ref[0], q3_ref.shape)
        ri = lax.broadcasted_iota(jnp.int32, q3_ref.shape, 1)
        ci = lax.broadcasted_iota(jnp.int32, q3_ref.shape, 2)
        q3_ref[...] = jnp.where((ci // QK_DIM) == ri, q, 0.0)
        mrun_ref[...] = jnp.full(sshape, NEG_BIG, F32)
        lrun_ref[...] = jnp.zeros(sshape, F32)
        acc_ref[...] = jnp.zeros(acc_ref.shape, F32)

    @pl.when(ph == 0)
    def _():
        q3 = q3_ref[...].astype(BF16)
        slope = slope_ref[...]
        tok = lax.broadcasted_iota(jnp.int32, sshape, 2)
        mrun = mrun_ref[...]
        for i in range(pp):
            pg = g * pp + i
            s = _bdot_nt(q3, k_refs[i][...].astype(BF16))
            dist = (past_len - (pg * page + tok)).astype(F32)
            s = s - slope * dist
            sc_ref[pg] = s
            mrun = jnp.maximum(mrun, s)
        mrun_ref[...] = mrun

    @pl.when((ph == 1) & (g == 0))
    def _():
        s_new = jnp.sum(q3_ref[...].astype(BF16).astype(F32) * kn_ref[0], axis=-1, keepdims=True)
        m = jnp.maximum(jnp.max(mrun_ref[...], axis=-1, keepdims=True), s_new)
        mrun_ref[...] = jnp.broadcast_to(m, sshape)
        snew_ref[...] = jnp.broadcast_to(s_new, sshape)

    @pl.when(ph == 1)
    def _():
        m = mrun_ref[...]
        lrun = lrun_ref[...]
        acc = acc_ref[...]
        for i in range(pp):
            pg = g * pp + i
            p = jnp.exp(sc_ref[pg] - m)
            lrun = lrun + p
            acc = acc + _bdot(p.astype(BF16), v_refs[i][...].astype(BF16))
        lrun_ref[...] = lrun
        acc_ref[...] = acc

    @pl.when((ph == 1) & (g == n_groups - 1))
    def _():
        m = mrun_ref[:, :, 0:1]
        p_new = jnp.exp(snew_ref[:, :, 0:1] - m)
        l = jnp.sum(lrun_ref[...], axis=-1, keepdims=True) + p_new
        normed = (acc_ref[...] + p_new * vn_ref[0]) / l
        lam = _lambda_value(lamv_ref[...], lam_init)
        o = normed[:, 0:1, :] - lam * normed[:, 1:2, :]
        o = o * lax.rsqrt(jnp.mean(o * o, axis=-1, keepdims=True) + EPS) * gain_ref[...] * (1.0 - lam_init)
        o_ref[0] = o * _silu(z_ref[0])


def _paged_attn(page_table, qn, kn, vn, zc, cache_k, cache_v, layer, slope3, lamv, gain, lam_init):
    bn, n_pages = page_table.shape
    page = cache_k.shape[3]
    pp = math.gcd(PAGES_PER_STEP, n_pages)
    n_groups = n_pages // pp
    past_len = n_pages * page
    head_row = pl.BlockSpec((1, C_HEADS, 1, HEAD_DIM), lambda b, ph, g, pt: (b, 0, 0, 0))

    def k_spec(i):
        def imap(b, ph, g, pt):
            grp = jnp.where(ph == 0, g, n_groups - 1)
            return (layer, pt[b, grp * pp + i], 0, 0, 0)
        return pl.BlockSpec((None, None, C_HEADS, page, HEAD_DIM), imap)

    def v_spec(i):
        def imap(b, ph, g, pt):
            grp = jnp.where(ph == 0, 0, g)
            return (layer, pt[b, grp * pp + i], 0, 0, 0)
        return pl.BlockSpec((None, None, C_HEADS, page, HEAD_DIM), imap)

    def full(a):
        nd = a.ndim
        return pl.BlockSpec(a.shape, lambda b, ph, g, pt: (0,) * nd)

    sshape = (C_HEADS, SCORE_ROWS, page)
    qshape = (C_HEADS, SCORE_ROWS, HEAD_DIM)
    kern = functools.partial(_paged_attn_kernel, lam_init=lam_init, past_len=past_len, n_groups=n_groups)
    grid_spec = pltpu.PrefetchScalarGridSpec(
        num_scalar_prefetch=1,
        grid=(bn, 2, n_groups),
        in_specs=[head_row, head_row, head_row, head_row, full(slope3), full(lamv), full(gain)]
                 + [k_spec(i) for i in range(pp)] + [v_spec(i) for i in range(pp)],
        out_specs=head_row,
        scratch_shapes=[pltpu.VMEM((n_pages,) + sshape, F32),
                        pltpu.VMEM(sshape, F32),
                        pltpu.VMEM(sshape, F32),
                        pltpu.VMEM(qshape, F32),
                        pltpu.VMEM(qshape, F32),
                        pltpu.VMEM(sshape, F32)])
    return pl.pallas_call(
        kern,
        grid_spec=grid_spec,
        out_shape=jax.ShapeDtypeStruct((bn, C_HEADS, 1, HEAD_DIM), F32),
        compiler_params=_cparams(3),
        name="paged_attn",
    )(page_table, qn, kn, vn, zc, slope3, lamv, gain, *([cache_k] * pp), *([cache_v] * pp))


def _pad_lanes(v, width=GATE_PAD):
    return jnp.pad(v.astype(F32), (0, width - v.shape[0])).reshape(1, width)


def _layer_params(l, norm_gain, chunk_w, chunk_b, gdn_conv_w, gdn_a_log, gdn_dt_bias,
                  gdn_norm_gain, attn_q_norm, attn_k_norm, lq1, lk1, lq2, lk2, subln):
    return dict(
        layer=l, norm_gain=norm_gain[l], chunk_w=chunk_w[l],
        bias_b=jnp.broadcast_to(chunk_b[l][:, :, None], (A_GROUPS, CHUNK_MLP, HEAD_DIM)),
        w00=jnp.repeat(chunk_w[l][:, 0, 0], HEAD_DIM).reshape(1, A_W),
        b0=jnp.repeat(chunk_b[l][:, 0], HEAD_DIM).reshape(1, A_W),
        conv_w=gdn_conv_w[l], alog=_pad_lanes(gdn_a_log[l]), dtb=_pad_lanes(gdn_dt_bias[l]),
        ggain=gdn_norm_gain[l].reshape(1, HEAD_DIM),
        gq2=jnp.tile(attn_q_norm[l], 2).reshape(1, HEAD_DIM),
        gk2=jnp.tile(attn_k_norm[l], 2).reshape(1, HEAD_DIM),
        lamv=jnp.stack([lq1[l], lk1[l], lq2[l], lk2[l]]).astype(F32),
        subln=subln[l].reshape(1, HEAD_DIM),
        lam_init=0.8 - 0.6 * math.exp(-0.3 * l),
    )


MAIN_TN = 512
MAIN_TM = 1024
GATING_ROWS = 2048


def _layer(xp, xs, lp, w_in_t, w_out, qaug, kaug, depth, kv_all, slope3, state, cbuf, cache_k, cache_v,
           page_table):
    bn, t, d = xp.shape
    sn = xs.shape[0]
    m = bn * t
    xp2 = xp.reshape(m, d)
    xs2 = xs.reshape(sn, d)
    tm = min(MAIN_TM, m)
    layer = lp['layer']
    h, gates, gates_s = _norm_gates(xp2, xs2, lp['norm_gain'], w_in_t, layer, min(512, m))
    proj, proj_s = _inproj(h, xs2, lp['norm_gain'], w_in_t, layer,
                           functools.partial(_main_row_start, tn=MAIN_TN), MAIN_COLS, tm, MAIN_TN)

    ps3 = proj_s.reshape(sn, 1, MAIN_COLS)
    a_s, a_v, b_s, s_new, qn, kn_s, vn, zc = _sample_mix(
        ps3, gates_s.reshape(sn, 1, GATE_PAD), cbuf, lp['conv_w'], state, lp['w00'], lp['b0'],
        lp['alog'], lp['dtb'], lp['ggain'], lp['gq2'], lp['gk2'])
    c_s = _paged_attn(page_table, qn, kn_s, vn, zc, cache_k, cache_v, layer, slope3,
                      lp['lamv'], lp['subln'], lp['lam_init'])
    conv_s = jnp.concatenate([cbuf[:, 1:], ps3[:, :, OFF_BQ:OFF_BQ + 3 * B_W]], axis=1)

    proj3 = proj.reshape(bn, t, MAIN_COLS)
    a_out = _gating_prompt(proj, lp['chunk_w'], lp['bias_b'], min(GATING_ROWS, t))
    b_out, s_fin = _gdn_prompt(proj3, gates.reshape(bn, t, GATE_PAD), lp['conv_w'], lp['alog'], lp['dtb'],
                               lp['ggain'])
    q1, q2, kn, k1, k2, vo, vb = _cprep_prompt(proj3, lp['gq2'], lp['gk2'], qaug, kaug, min(256, t),
                                               layer, depth, kv_all)
    c_out = _attn_prompt(q1, q2, k1, k2, vb, proj3, lp['lamv'], lp['subln'], lp['lam_init'],
                         min(ATTN_BLOCK, t))
    conv_p = proj3[:, t - (CONV_WIDTH - 1):, OFF_BQ:OFF_BQ + 3 * B_W]

    yp, ys = _outproj(xp2, a_out, b_out.reshape(m, B_W), c_out.reshape(m, C_W),
                      xs2, a_s.reshape(sn, A_W), b_s.reshape(sn, B_W), c_s.reshape(sn, C_W),
                      w_out, layer, tm, MAIN_TN)
    prompt_out = (yp.reshape(bn, t, d), (kn, vo), s_fin, conv_p)
    sample_out = (ys.reshape(sn, 1, d), kn_s.reshape(sn, 1, C_HEADS, HEAD_DIM), vn.reshape(sn, 1, C_HEADS, HEAD_DIM),
                  s_new, conv_s, a_v)
    return prompt_out, sample_out


def kernel(x_prompt, x_sample, cache_attn_k, cache_attn_v, state_gdn, state_gdn_conv, page_table,
           norm_gain, w_in, w_out, chunk_w, chunk_b, gdn_conv_w, gdn_a_log, gdn_dt_bias, gdn_norm_gain,
           attn_q_norm, attn_k_norm, lambda_q1, lambda_k1, lambda_q2, lambda_k2, attn_subln_gain):
    depth = w_in.shape[0]
    slopes_np = _alibi_slopes(C_HEADS)
    page = cache_attn_k.shape[2]
    slope3 = jnp.asarray(np.broadcast_to(slopes_np[:, None, None], (C_HEADS, SCORE_ROWS, page)).copy())
    qaug, kaug = _alibi_tables(slopes_np, x_prompt.shape[1])
    w_in_t = jnp.transpose(w_in, (0, 2, 1))
    ck = jnp.transpose(cache_attn_k, (0, 1, 3, 2, 4))
    cv = jnp.transpose(cache_attn_v, (0, 1, 3, 2, 4))

    yp, ys = x_prompt, x_sample
    outs = [[] for _ in range(7)]
    kv_all = None
    for l in range(depth):
        lp = _layer_params(l, norm_gain, chunk_w, chunk_b, gdn_conv_w, gdn_a_log, gdn_dt_bias,
                           gdn_norm_gain, attn_q_norm, attn_k_norm, lambda_q1, lambda_k1, lambda_q2,
                           lambda_k2, attn_subln_gain)
        (yp, kv_all, ps, pc), (ys, sk, sv, ss, sc, sa) = _layer(
            yp, ys, lp, w_in_t, w_out, qaug, kaug, depth, kv_all, slope3, state_gdn[l], state_gdn_conv[l],
            ck, cv, page_table)
        for lst, val in zip(outs, (ps, pc, sk, sv, ss, sc, sa)):
            lst.append(val)
    res = [jnp.stack(o) for o in outs]
    pk = jnp.transpose(kv_all[0], (0, 1, 3, 2, 4))
    pv = jnp.transpose(kv_all[1], (0, 1, 3, 2, 4))
    return (yp, ys, pk, pv) + tuple(res)
```

```python
import functools
import math

import jax
import jax.numpy as jnp
import numpy as np
from jax import lax
from jax.experimental import pallas as pl
from jax.experimental.pallas import tpu as pltpu

F32 = jnp.float32
BF16 = jnp.bfloat16
HIGHEST = lax.Precision.HIGHEST

HEAD_DIM = 128
A_GROUPS = 8
B_HEADS = 12
C_HEADS = 12
A_W = A_GROUPS * HEAD_DIM
B_W = B_HEADS * HEAD_DIM
C_W = C_HEADS * HEAD_DIM
QK_DIM = HEAD_DIM // 2
CHUNK_MLP = 128
GDN_CHUNK = 64
CONV_WIDTH = 4
EPS = 1e-6
INV_SQRT2 = 0.7071067811865476
NEG_BIG = -1e30

OFF_AU, OFF_AV, OFF_AZ = 0, A_W, 2 * A_W
OFF_BQ = 3 * A_W
OFF_BK = OFF_BQ + B_W
OFF_BV = OFF_BK + B_W
OFF_BZ = OFF_BV + B_W
OFF_CQ = OFF_BZ + B_W
OFF_CK = OFF_CQ + C_W
OFF_CV = OFF_CK + C_W
OFF_CZ = OFF_CV + C_W
MAIN_COLS = OFF_CZ + C_W
GATE_SRC = 3 * A_W + 4 * B_W
GATE_PAD = 128

VMEM_LIMIT = 56 * 1024 * 1024
PAGES_PER_STEP = 8
SCORE_ROWS = 8


def _cparams(n_axes):
    return pltpu.CompilerParams(dimension_semantics=("arbitrary",) * n_axes,
                                vmem_limit_bytes=VMEM_LIMIT)


def _gelu(x):
    return 0.5 * x * (1.0 + lax.erf(x * INV_SQRT2))


def _silu(x):
    return x * jax.nn.sigmoid(x)


def _dot(a, b):
    return jnp.dot(a, b, preferred_element_type=F32)


def _dot_hi(a, b):
    return jnp.dot(a, b, preferred_element_type=F32, precision=HIGHEST)


def _dot_nt(a, b):
    return lax.dot_general(a, b, (((1,), (1,)), ((), ())), preferred_element_type=F32)


def _dot_tn(a, b, precision=None):
    return lax.dot_general(a, b, (((0,), (0,)), ((), ())), preferred_element_type=F32,
                           precision=precision)


def _bdot_nt(a, b):
    return lax.dot_general(a, b, (((2,), (2,)), ((0,), (0,))), preferred_element_type=F32)


def _bdot(a, b):
    return lax.dot_general(a, b, (((2,), (1,)), ((0,), (0,))), preferred_element_type=F32)


def _alibi_slopes(n):
    def pow2(m):
        start = 2.0 ** (-8.0 / m)
        return [start ** (i + 1) for i in range(m)]
    p = 2 ** int(math.floor(math.log2(n)))
    s = pow2(p)
    if p < n:
        s = s + pow2(2 * p)[0::2][: n - p]
    return np.array(s, dtype=np.float32)


def _rms_bf16(x, gain):
    ms = jnp.mean(x * x, axis=-1, keepdims=True)
    return (x * lax.rsqrt(ms + EPS) * gain).astype(BF16)


def _norm_gates_kernel(x_ref, g_ref, w_ref, xs_ref, h_ref, o_ref, os_ref, wt_ref):
    @pl.when(pl.program_id(0) == 0)
    def _():
        wt_ref[...] = w_ref[0].astype(BF16).T
        os_ref[...] = _dot(_rms_bf16(xs_ref[...], g_ref[...]), wt_ref[...])
    h = _rms_bf16(x_ref[...], g_ref[...])
    h_ref[...] = h
    o_ref[...] = _dot(h, wt_ref[...])


def _norm_gates(x, xs, gain, w_t, layer, tm):
    m, d = x.shape
    ms = xs.shape[0]
    return pl.pallas_call(
        _norm_gates_kernel,
        grid=(m // tm,),
        in_specs=[pl.BlockSpec((tm, d), lambda i: (i, 0)),
                  pl.BlockSpec((1, d), lambda i: (0, 0)),
                  pl.BlockSpec((pl.Element(1), pl.Element(GATE_PAD), pl.Element(d)),
                               lambda i: (layer, GATE_SRC, 0)),
                  pl.BlockSpec((ms, d), lambda i: (0, 0))],
        out_specs=[pl.BlockSpec((tm, d), lambda i: (i, 0)),
                   pl.BlockSpec((tm, GATE_PAD), lambda i: (i, 0)),
                   pl.BlockSpec((ms, GATE_PAD), lambda i: (0, 0))],
        out_shape=[jax.ShapeDtypeStruct((m, d), BF16),
                   jax.ShapeDtypeStruct((m, GATE_PAD), F32),
                   jax.ShapeDtypeStruct((ms, GATE_PAD), F32)],
        scratch_shapes=[pltpu.VMEM((d, GATE_PAD), BF16)],
        compiler_params=_cparams(1),
        name="norm_gates",
    )(x, gain.reshape(1, d), w_t, xs)


def _main_row_start(j, tn):
    return pl.multiple_of(j * tn + jnp.where(j * tn >= GATE_SRC, 2 * B_HEADS, 0), 8)


def _wt_spec(layer, tn, k, row_start):
    return pl.BlockSpec((pl.Element(1), pl.Element(tn), pl.Element(k)),
                        lambda j, i: (layer, row_start(j), 0))


def _inproj_kernel(a_ref, w_ref, xs_ref, g_ref, o_ref, os_ref, wt_ref):
    @pl.when(pl.program_id(1) == 0)
    def _():
        wt_ref[...] = w_ref[0].astype(BF16).T
        os_ref[...] = _dot(_rms_bf16(xs_ref[...], g_ref[...]), wt_ref[...])
    o_ref[...] = _dot(a_ref[...], wt_ref[...])


def _inproj(a, xs, gain, w_t, layer, row_start, n_out, tm, tn):
    m, k = a.shape
    ms = xs.shape[0]
    return pl.pallas_call(
        _inproj_kernel,
        grid=(n_out // tn, m // tm),
        in_specs=[pl.BlockSpec((tm, k), lambda j, i: (i, 0)),
                  _wt_spec(layer, tn, k, row_start),
                  pl.BlockSpec((ms, k), lambda j, i: (0, 0)),
                  pl.BlockSpec((1, k), lambda j, i: (0, 0))],
        out_specs=[pl.BlockSpec((tm, tn), lambda j, i: (i, j)),
                   pl.BlockSpec((ms, tn), lambda j, i: (0, j))],
        out_shape=[jax.ShapeDtypeStruct((m, n_out), F32),
                   jax.ShapeDtypeStruct((ms, n_out), F32)],
        scratch_shapes=[pltpu.VMEM((k, tn), BF16)],
        compiler_params=_cparams(2),
        name="in_proj",
    )(a, w_t, xs, gain.reshape(1, k))


def _mix_dot(a_ref, b_ref, c_ref, wb_ref):
    ka = a_ref.shape[1]
    kb = ka + b_ref.shape[1]
    acc = _dot(a_ref[...].astype(BF16), wb_ref[0:ka, :])
    acc = acc + _dot(b_ref[...].astype(BF16), wb_ref[ka:kb, :])
    return acc + _dot(c_ref[...].astype(BF16), wb_ref[kb:, :])


def _outproj_kernel(x_ref, a_ref, b_ref, c_ref, w_ref, xs_ref, as_ref, bs_ref, cs_ref,
                    o_ref, os_ref, wb_ref):
    @pl.when(pl.program_id(1) == 0)
    def _():
        wb_ref[...] = w_ref[...].astype(BF16)
        os_ref[...] = xs_ref[...] + _mix_dot(as_ref, bs_ref, cs_ref, wb_ref)
    o_ref[...] = x_ref[...] + _mix_dot(a_ref, b_ref, c_ref, wb_ref)


def _outproj(x, a, b, c, xs, a_s, b_s, c_s, w_out, layer, tm, tn):
    m, d = x.shape
    ms = xs.shape[0]
    k = w_out.shape[1]
    rows = lambda arr: pl.BlockSpec((tm, arr.shape[1]), lambda j, i: (i, 0))
    whole = lambda arr: pl.BlockSpec(arr.shape, lambda j, i: (0, 0))
    return pl.pallas_call(
        _outproj_kernel,
        grid=(d // tn, m // tm),
        in_specs=[pl.BlockSpec((tm, tn), lambda j, i: (i, j)), rows(a), rows(b), rows(c),
                  pl.BlockSpec((None, k, tn), lambda j, i: (layer, 0, j)),
                  pl.BlockSpec((ms, tn), lambda j, i: (0, j)), whole(a_s), whole(b_s), whole(c_s)],
        out_specs=[pl.BlockSpec((tm, tn), lambda j, i: (i, j)),
                   pl.BlockSpec((ms, tn), lambda j, i: (0, j))],
        out_shape=[jax.ShapeDtypeStruct((m, d), F32), jax.ShapeDtypeStruct((ms, d), F32)],
        scratch_shapes=[pltpu.VMEM((k, tn), BF16)],
        compiler_params=_cparams(2),
        name="out_proj",
    )(x, a, b, c, w_out, xs, a_s, b_s, c_s)


def _gating_kernel(u_ref, v_ref, z_ref, w_ref, b_ref, o_ref):
    rows = u_ref.shape[0]
    ri = lax.broadcasted_iota(jnp.int32, (CHUNK_MLP, CHUNK_MLP), 0)
    ci = lax.broadcasted_iota(jnp.int32, (CHUNK_MLP, CHUNK_MLP), 1)
    w = jnp.where(ri >= ci, w_ref[0], 0.0).astype(BF16)
    bias = b_ref[0]
    for c in range(rows // CHUNK_MLP):
        sl = slice(c * CHUNK_MLP, (c + 1) * CHUNK_MLP)
        vg = _gelu(v_ref[sl, :]).astype(BF16)
        mixed = _dot(w, vg) + bias
        o_ref[sl, :] = (_gelu(u_ref[sl, :]) * mixed * _silu(z_ref[sl, :])).astype(o_ref.dtype)


def _gating_prompt(proj, chunk_w, bias_b, rows):
    m = proj.shape[0]
    nb = HEAD_DIM
    return pl.pallas_call(
        _gating_kernel,
        grid=(A_GROUPS, m // rows),
        in_specs=[pl.BlockSpec((rows, nb), lambda g, i: (i, OFF_AU // nb + g)),
                  pl.BlockSpec((rows, nb), lambda g, i: (i, OFF_AV // nb + g)),
                  pl.BlockSpec((rows, nb), lambda g, i: (i, OFF_AZ // nb + g)),
                  pl.BlockSpec((1, CHUNK_MLP, CHUNK_MLP), lambda g, i: (g, 0, 0)),
                  pl.BlockSpec((1, CHUNK_MLP, nb), lambda g, i: (g, 0, 0))],
        out_specs=pl.BlockSpec((rows, nb), lambda g, i: (i, g)),
        out_shape=jax.ShapeDtypeStruct((m, A_W), BF16),
        compiler_params=_cparams(2),
        name="gating_prompt",
    )(proj, proj, proj, chunk_w, bias_b)


GDN_HEADS_PER_STEP = 12
GDN_ROWS_PER_STEP = 512
NEUMANN_SPLIT_STEPS = 2


def _split2(a):
    hi = a.astype(BF16)
    lo = (a - hi.astype(F32)).astype(BF16)
    return hi, lo


def _dot3(ah, al, bh, bl):
    return _dot(ah, bh) + (_dot(ah, bl) + _dot(al, bh))


def _gdn_prompt_kernel(q_ref, k_ref, v_ref, z_ref, ab_ref, wq_ref, wk_ref, wv_ref,
                       alog_ref, dtb_ref, gain_ref, o_ref, s_ref, st_ref, tail_ref):
    hg = pl.program_id(1)
    tb = pl.program_id(2)
    rows = q_ref.shape[1]
    heads = GDN_HEADS_PER_STEP
    pair = 2 * GDN_CHUNK

    ri = lax.broadcasted_iota(jnp.int32, (pair, pair), 0)
    ci = lax.broadcasted_iota(jnp.int32, (pair, pair), 1)
    same = (ri >= GDN_CHUNK) == (ci >= GDN_CHUNK)
    incl = same & (ri >= ci)
    strict = same & (ri > ci)
    incl16 = jnp.where(incl, 1.0, 0.0).astype(BF16)
    eye = jnp.where(ri == ci, 1.0, 0.0)
    row_lo = lax.broadcasted_iota(jnp.int32, (pair, HEAD_DIM), 0) < GDN_CHUNK
    gain = gain_ref[...]

    @pl.when(tb == 0)
    def _():
        st_ref[...] = jnp.zeros_like(st_ref)
        tail_ref[...] = jnp.zeros_like(tail_ref)

    hs = range(heads)
    cols = [slice(g * HEAD_DIM, (g + 1) * HEAD_DIM) for g in hs]
    pick_a = [jnp.where(ri == (hg * heads + g), 1.0, 0.0).astype(BF16) for g in hs]
    pick_b = [jnp.where(ri == (B_HEADS + hg * heads + g), 1.0, 0.0).astype(BF16) for g in hs]
    neg_a_row = -jnp.exp(alog_ref[...])
    dtb_row = dtb_ref[...]

    def split3(x):
        hi = x.astype(BF16)
        r = x - hi.astype(F32)
        mid = r.astype(BF16)
        return hi, mid, (r - mid.astype(F32)).astype(BF16)

    def body(i, carry):
        r0 = pl.multiple_of(i * pair, pair)
        rp = pl.multiple_of(jnp.maximum(r0 - 8, 0), 8)
        ab = ab_ref[0, pl.ds(r0, pair), :]

        def conv(x_ref, w_ref, part, g):
            cur = x_ref[0, pl.ds(r0, pair), cols[g]]
            prev = jnp.where(i > 0, x_ref[0, pl.ds(rp, 8), cols[g]], tail_ref[part, :, cols[g]])
            w = w_ref[:, cols[g]]
            head = jnp.concatenate([prev, cur[0:8]], axis=0)
            acc = cur * w[CONV_WIDTH - 1:CONV_WIDTH]
            for s in range(1, CONV_WIDTH):
                shifted = jnp.concatenate([head[8 - s:16 - s], pltpu.roll(cur, s, 0)[8:]], axis=0)
                acc = acc + shifted * w[CONV_WIDTH - 1 - s:CONV_WIDTH - s]
            return _silu(acc)

        def l2n(x):
            return x * lax.rsqrt(jnp.sum(x * x, axis=-1, keepdims=True) + EPS)

        q = [l2n(conv(q_ref, wq_ref, 0, g)) * (HEAD_DIM ** -0.5) for g in hs]
        k = [l2n(conv(k_ref, wk_ref, 1, g)) for g in hs]
        v = [conv(v_ref, wv_ref, 2, g) for g in hs]
        g_all = neg_a_row * jax.nn.softplus(ab + dtb_row)
        gp = split3(g_all)
        gc_all = _dot(incl16, gp[0]) + (_dot(incl16, gp[1]) + _dot(incl16, gp[2]))
        cp = split3(gc_all)
        bp = _split2(jax.nn.sigmoid(ab))
        gcb = [_dot(cp[0], pick_a[g]) + (_dot(cp[1], pick_a[g]) + _dot(cp[2], pick_a[g])) for g in hs]
        beta = [_dot(bp[0], pick_b[g]) + _dot(bp[1], pick_b[g]) for g in hs]
        egc = [jnp.exp(x) for x in gcb]
        glast = [jnp.where(row_lo, x[GDN_CHUNK - 1:GDN_CHUNK, :], x[pair - 1:pair, :]) for x in gcb]
        kdec = [k[g] * jnp.exp(glast[g] - gcb[g]) for g in hs]
        decay = [jnp.where(incl, jnp.exp(jnp.where(incl, x - x.T, 0.0)), 0.0) for x in gcb]

        kb = [k[g] * beta[g] for g in hs]
        k16 = [x.astype(BF16) for x in k]
        lmat = [jnp.where(strict, _dot_nt(kb[g].astype(BF16), k16[g]) * decay[g], 0.0) for g in hs]
        tinv = [eye - x for x in lmat]
        lsp = [_split2(x) for x in lmat]
        pw = [_dot3(lh, ll, lh, ll) for lh, ll in lsp]
        for step in range(5):
            if step < NEUMANN_SPLIT_STEPS:
                psp = [_split2(x) for x in pw]
                tsp = [_split2(x) for x in tinv]
                tinv = [tinv[g] + _dot3(tsp[g][0], tsp[g][1], psp[g][0], psp[g][1]) for g in hs]
                pw = [_dot3(ph, pl_, ph, pl_) for ph, pl_ in psp]
            else:
                p16 = [x.astype(BF16) for x in pw]
                tinv = [tinv[g] + _dot(tinv[g].astype(BF16), p16[g]) for g in hs]
                if step < 4:
                    pw = [_dot(x, x) for x in p16]
        t16 = [x.astype(BF16) for x in tinv]
        u = [_dot(t16[g], (v[g] * beta[g]).astype(BF16)) for g in hs]
        w16 = [_dot(t16[g], (kb[g] * egc[g]).astype(BF16)).astype(BF16) for g in hs]
        a_intra = [(_dot_nt(q[g].astype(BF16), k16[g]) * decay[g]).astype(BF16) for g in hs]
        qdec = [(q[g] * egc[g]).astype(BF16) for g in hs]

        s = [st_ref[g] for g in hs]
        outs = [[] for _ in hs]
        for c in range(2):
            sl = slice(c * GDN_CHUNK, (c + 1) * GDN_CHUNK)
            in_chunk = row_lo if c == 0 else jnp.logical_not(row_lo)
            s16 = [x.astype(BF16) for x in s]
            vn16 = [jnp.where(in_chunk, u[g] - _dot(w16[g], s16[g]), 0.0).astype(BF16) for g in hs]
            for g in hs:
                outs[g].append(_dot(qdec[g][sl], s16[g]) + _dot(a_intra[g][sl], vn16[g]))
            gt = [jnp.exp(x[(c + 1) * GDN_CHUNK - 1:(c + 1) * GDN_CHUNK, :]) for x in gcb]
            kd = [jnp.where(in_chunk, x, 0.0).astype(BF16) for x in kdec]
            s = [s[g] * gt[g] + _dot_tn(kd[g], vn16[g]) for g in hs]
        for g in hs:
            st_ref[g] = s[g]
            o = jnp.concatenate(outs[g], axis=0)
            o = o * lax.rsqrt(jnp.mean(o * o, axis=-1, keepdims=True) + EPS) * gain
            o_ref[0, pl.ds(r0, pair), cols[g]] = (
                o * _silu(z_ref[0, pl.ds(r0, pair), cols[g]])).astype(o_ref.dtype)
        return carry

    lax.fori_loop(0, rows // pair, body, 0)
    for part, x_ref in enumerate((q_ref, k_ref, v_ref)):
        tail_ref[part] = x_ref[0, rows - 8:rows, :]

    @pl.when(tb == pl.num_programs(2) - 1)
    def _():
        s_ref[0] = st_ref[...]


def _gdn_prompt(proj3, ab3, conv_w, alog, dtb, gain):
    bn, t, _ = proj3.shape
    heads = GDN_HEADS_PER_STEP
    rows = min(GDN_ROWS_PER_STEP, t)
    width = heads * HEAD_DIM
    n_hg = B_HEADS // heads
    col = lambda off: (lambda b, hg, tb: (b, tb, off // width + hg))
    wcol = lambda part: (lambda b, hg, tb: (0, part * n_hg + hg))
    vec = pl.BlockSpec((1, HEAD_DIM), lambda b, hg, tb: (0, 0))
    return pl.pallas_call(
        _gdn_prompt_kernel,
        grid=(bn, n_hg, t // rows),
        in_specs=[pl.BlockSpec((1, rows, width), col(OFF_BQ)),
                  pl.BlockSpec((1, rows, width), col(OFF_BK)),
                  pl.BlockSpec((1, rows, width), col(OFF_BV)),
                  pl.BlockSpec((1, rows, width), col(OFF_BZ)),
                  pl.BlockSpec((1, rows, GATE_PAD), lambda b, hg, tb: (b, tb, 0)),
                  pl.BlockSpec((CONV_WIDTH, width), wcol(0)),
                  pl.BlockSpec((CONV_WIDTH, width), wcol(1)),
                  pl.BlockSpec((CONV_WIDTH, width), wcol(2)),
                  vec, vec, vec],
        out_specs=[pl.BlockSpec((1, rows, width), lambda b, hg, tb: (b, tb, hg)),
                   pl.BlockSpec((1, heads, HEAD_DIM, HEAD_DIM), lambda b, hg, tb: (b, hg, 0, 0))],
        out_shape=[jax.ShapeDtypeStruct((bn, t, B_W), BF16),
                   jax.ShapeDtypeStruct((bn, B_HEADS, HEAD_DIM, HEAD_DIM), F32)],
        scratch_shapes=[pltpu.VMEM((heads, HEAD_DIM, HEAD_DIM), F32),
                        pltpu.VMEM((3, 8, width), F32)],
        compiler_params=_cparams(3),
        name="gdn_prompt",
    )(proj3, proj3, proj3, proj3, ab3, conv_w, conv_w, conv_w, alog, dtb, gain)


AUG_LANE0 = QK_DIM
POS_SPLIT = 64
ATTN_BLOCK = 1024
ATTN_HEADS_PER_STEP = 1


def _half_rmsnorm(x, gain2):
    lo = lax.broadcasted_iota(jnp.int32, x.shape, x.ndim - 1) < QK_DIM
    x2 = x * x
    s_lo = jnp.sum(jnp.where(lo, x2, 0.0), axis=-1, keepdims=True)
    s_hi = jnp.sum(jnp.where(lo, 0.0, x2), axis=-1, keepdims=True)
    ms = jnp.where(lo, s_lo, s_hi) * (1.0 / QK_DIM)
    return x * lax.rsqrt(ms + EPS) * gain2


def _alibi_tables(slopes_np, t):
    import ml_dtypes
    bf = ml_dtypes.bfloat16
    hi = slopes_np.astype(bf).astype(np.float32)
    mid = (slopes_np - hi).astype(bf).astype(np.float32)
    lo = (slopes_np - hi - mid).astype(bf).astype(np.float32)
    qaug = np.zeros((C_HEADS, HEAD_DIM), np.float32)
    for n, piece in enumerate((hi, mid, lo)):
        qaug[:, AUG_LANE0 + n] = POS_SPLIT * piece
        qaug[:, AUG_LANE0 + 3 + n] = piece
    pos = np.arange(t)
    kaug = np.zeros((t, HEAD_DIM), np.float32)
    kaug[:, AUG_LANE0:AUG_LANE0 + 3] = (pos // POS_SPLIT)[:, None]
    kaug[:, AUG_LANE0 + 3:AUG_LANE0 + 6] = (pos % POS_SPLIT)[:, None]
    return jnp.asarray(qaug.reshape(1, C_W)), jnp.asarray(kaug)


def _cprep_kernel(q_ref, k_ref, v_ref, gq_ref, gk_ref, qaug_ref, kaug_ref, *rest):
    q1_ref, q2_ref, kn_ref, k1_ref, k2_ref, vo_ref, vb_ref = rest[-7:]
    gq = gq_ref[...]
    gk = gk_ref[...]
    kaug = kaug_ref[...]
    lo = lax.broadcasted_iota(jnp.int32, kaug.shape, 1) < QK_DIM
    for h in range(C_HEADS):
        sl = slice(h * HEAD_DIM, (h + 1) * HEAD_DIM)
        qn = _half_rmsnorm(q_ref[0, :, sl], gq) * (QK_DIM ** -0.5)
        kn = _half_rmsnorm(k_ref[0, :, sl], gk)
        qaug = qaug_ref[:, sl]
        q1_ref[0, :, sl] = jnp.where(lo, qn, qaug).astype(BF16)
        q2_ref[0, :, sl] = jnp.where(lo, pltpu.roll(qn, QK_DIM, 1), qaug).astype(BF16)
        kn_ref[0, h] = kn
        k1_ref[0, :, sl] = jnp.where(lo, kn, kaug).astype(BF16)
        k2_ref[0, :, sl] = jnp.where(lo, pltpu.roll(kn, QK_DIM, 1), kaug).astype(BF16)
        vo_ref[0, h] = v_ref[0, :, sl]
    vb_ref[...] = v_ref[...].astype(BF16)


def _cprep_prompt(proj3, gq2, gk2, qaug, kaug, tr, layer, depth, kv_all):
    bn, t, _ = proj3.shape
    blk = lambda off: pl.BlockSpec((1, tr, C_W), lambda b, r: (b, r, off // C_W))
    out = pl.BlockSpec((1, tr, C_W), lambda b, r: (b, r, 0))
    out_hm = pl.BlockSpec((None, 1, C_HEADS, tr, HEAD_DIM), lambda b, r: (layer, b, 0, r, 0))
    vec = pl.BlockSpec((1, HEAD_DIM), lambda b, r: (0, 0))
    sd = lambda dt: jax.ShapeDtypeStruct((bn, t, C_W), dt)
    sd_hm = jax.ShapeDtypeStruct((depth, bn, C_HEADS, t, HEAD_DIM), F32)
    in_specs = [blk(OFF_CQ), blk(OFF_CK), blk(OFF_CV), vec, vec,
                pl.BlockSpec((1, C_W), lambda b, r: (0, 0)),
                pl.BlockSpec((tr, HEAD_DIM), lambda b, r: (r, 0))]
    args = [proj3, proj3, proj3, gq2, gk2, qaug, kaug]
    aliases = {}
    if kv_all is not None:
        in_specs += [pl.BlockSpec(memory_space=pl.ANY)] * 2
        aliases = {len(args): 2, len(args) + 1: 5}
        args += list(kv_all)
    return pl.pallas_call(
        _cprep_kernel,
        grid=(bn, t // tr),
        in_specs=in_specs,
        out_specs=[out, out, out_hm, out, out, out_hm, out],
        out_shape=[sd(BF16), sd(BF16), sd_hm, sd(BF16), sd(BF16), sd_hm, sd(BF16)],
        input_output_aliases=aliases,
        compiler_params=_cparams(2),
        name="cprep_prompt",
    )(*args)


def _lambda_value(lamv, lam_init):
    e1 = jnp.exp(jnp.sum(lamv[0:1] * lamv[1:2], axis=-1, keepdims=True))
    e2 = jnp.exp(jnp.sum(lamv[2:3] * lamv[3:4], axis=-1, keepdims=True))
    return e1 - e2 + lam_init


def _attn_prompt_kernel(q1_ref, q2_ref, k1_ref, k2_ref, v_ref, z_ref, lamv_ref, gain_ref, o_ref,
                        *, lam_init, tq):
    qi = pl.program_id(2)
    heads = q1_ref.shape[2] // HEAD_DIM
    cols = [slice(g * HEAD_DIM, (g + 1) * HEAD_DIM) for g in range(heads)]
    chains = [(g, c) for g in range(heads) for c in range(2)]
    nc = len(chains)
    q = [(q1_ref, q2_ref)[c][0, :, cols[g]] for g, c in chains]
    causal = (lax.broadcasted_iota(jnp.int32, (tq, tq), 1) <= lax.broadcasted_iota(jnp.int32, (tq, tq), 0))

    def step(j, carry, masked):
        m, l, acc = carry[0::3], carry[1::3], carry[2::3]
        c0 = pl.multiple_of(j * tq, tq)
        vblk = [v_ref[0, pl.ds(c0, tq), cols[g]] for g in range(heads)]
        s = [_dot_nt(q[n], (k1_ref, k2_ref)[c][0, pl.ds(c0, tq), cols[g]]) for n, (g, c) in enumerate(chains)]
        if masked:
            s = [jnp.where(causal, x, NEG_BIG) for x in s]
        m_new = [jnp.maximum(m[n], jnp.max(s[n], axis=-1, keepdims=True)) for n in range(nc)]
        alpha = [jnp.exp(m[n] - m_new[n]) for n in range(nc)]
        p = [jnp.exp(s[n] - m_new[n]) for n in range(nc)]
        l = [alpha[n] * l[n] + jnp.sum(p[n], axis=-1, keepdims=True) for n in range(nc)]
        pv = [_dot(p[n].astype(BF16), vblk[g]) for n, (g, c) in enumerate(chains)]
        acc = [alpha[n] * acc[n] + pv[n] for n in range(nc)]
        out = []
        for n in range(nc):
            out += [m_new[n], l[n], acc[n]]
        return tuple(out)

    m0 = jnp.full((tq, 1), NEG_BIG, F32)
    l0 = jnp.zeros((tq, 1), F32)
    a0 = jnp.zeros((tq, HEAD_DIM), F32)
    carry = lax.fori_loop(0, qi, lambda j, c: step(j, c, False), (m0, l0, a0) * nc)
    carry = step(qi, carry, True)
    l, acc = carry[1::3], carry[2::3]
    lam = _lambda_value(lamv_ref[...], lam_init)
    for g in range(heads):
        o = acc[2 * g] / l[2 * g] - lam * (acc[2 * g + 1] / l[2 * g + 1])
        o = o * lax.rsqrt(jnp.mean(o * o, axis=-1, keepdims=True) + EPS) * gain_ref[...] * (1.0 - lam_init)
        o_ref[0, :, cols[g]] = (o * _silu(z_ref[0, :, cols[g]])).astype(o_ref.dtype)


def _attn_prompt(q1, q2, k1, k2, vb, proj3, lamv, gain, lam_init, tq):
    bn, t, _ = q1.shape
    nb = ATTN_HEADS_PER_STEP * HEAD_DIM
    kern = functools.partial(_attn_prompt_kernel, lam_init=lam_init, tq=tq)
    qspec = pl.BlockSpec((1, tq, nb), lambda b, h, i: (b, i, h))
    kspec = pl.BlockSpec((1, t, nb), lambda b, h, i: (b, 0, h))
    return pl.pallas_call(
        kern,
        grid=(bn, C_HEADS // ATTN_HEADS_PER_STEP, t // tq),
        in_specs=[qspec, qspec, kspec, kspec, kspec,
                  pl.BlockSpec((1, tq, nb), lambda b, h, i: (b, i, OFF_CZ // nb + h)),
                  pl.BlockSpec((4, QK_DIM), lambda b, h, i: (0, 0)),
                  pl.BlockSpec((1, HEAD_DIM), lambda b, h, i: (0, 0))],
        out_specs=pl.BlockSpec((1, tq, nb), lambda b, h, i: (b, i, h)),
        out_shape=jax.ShapeDtypeStruct((bn, t, C_W), BF16),
        compiler_params=_cparams(3),
        name="attn_prompt",
    )(q1, q2, k1, k2, vb, proj3, lamv, gain)


def _sample_mix_kernel(p_ref, ab_ref, cbuf_ref, cw_ref, st_ref, w00_ref, b0_ref, alog_ref, dtb_ref,
                       ggain_ref, gq_ref, gk_ref,
                       aout_ref, av_ref, bout_ref, snew_ref, qn_ref, kn_ref, vn_ref, zc_ref):
    nb = HEAD_DIM
    p = p_ref[0]
    av = _gelu(p[:, OFF_AV:OFF_AV + A_W])
    av_ref[0] = av
    mixed = w00_ref[...] * av + b0_ref[...]
    aout_ref[0] = _gelu(p[:, OFF_AU:OFF_AU + A_W]) * mixed * _silu(p[:, OFF_AZ:OFF_AZ + A_W])

    cb = cbuf_ref[0]
    cw = cw_ref[...]
    x = p[:, OFF_BQ:OFF_BQ + 3 * B_W]
    acc = cb[0:1] * cw[0:1] + cb[1:2] * cw[1:2] + cb[2:3] * cw[2:3] + x * cw[3:4]
    act = _silu(acc)
    ab = ab_ref[0]
    g_row = -jnp.exp(alog_ref[...]) * jax.nn.softplus(ab + dtb_ref[...])
    beta_row = jax.nn.sigmoid(ab)
    ggain = ggain_ref[...]
    row8 = lax.broadcasted_iota(jnp.int32, (8, nb), 0)
    for h in range(B_HEADS):
        q = act[:, h * nb:(h + 1) * nb]
        k = act[:, B_W + h * nb:B_W + (h + 1) * nb]
        v = act[:, 2 * B_W + h * nb:2 * B_W + (h + 1) * nb]
        q = q * lax.rsqrt(jnp.sum(q * q, axis=-1, keepdims=True) + EPS) * (nb ** -0.5)
        k = k * lax.rsqrt(jnp.sum(k * k, axis=-1, keepdims=True) + EPS)
        eg = jnp.exp(g_row[:, h:h + 1])
        beta = beta_row[:, B_HEADS + h:B_HEADS + h + 1]
        s = st_ref[0, h]
        lhs = jnp.where(row8 == 0, k * (beta * eg), jnp.where(row8 == 1, q * eg, 0.0))
        rs = _dot_hi(lhs, s)
        v_new = v * beta - rs[0:1]
        o = rs[1:2] + jnp.sum(q * k, axis=-1, keepdims=True) * v_new
        k8 = jnp.where(row8 == 0, k, 0.0)
        v8 = jnp.where(row8 == 0, v_new, 0.0)
        snew_ref[0, h] = s * eg + _dot_tn(k8, v8, precision=HIGHEST)
        o = o * lax.rsqrt(jnp.mean(o * o, axis=-1, keepdims=True) + EPS) * ggain
        bout_ref[0, :, h * nb:(h + 1) * nb] = o * _silu(p[:, OFF_BZ + h * nb:OFF_BZ + (h + 1) * nb])

    gq = gq_ref[...]
    gk = gk_ref[...]
    for h in range(C_HEADS):
        qn_ref[0, h] = _half_rmsnorm(p[:, OFF_CQ + h * nb:OFF_CQ + (h + 1) * nb], gq) * (QK_DIM ** -0.5)
        kn_ref[0, h] = _half_rmsnorm(p[:, OFF_CK + h * nb:OFF_CK + (h + 1) * nb], gk)
        vn_ref[0, h] = p[:, OFF_CV + h * nb:OFF_CV + (h + 1) * nb]
        zc_ref[0, h] = p[:, OFF_CZ + h * nb:OFF_CZ + (h + 1) * nb]


def _sample_mix(proj3, ab3, cbuf, conv_w, state, w00, b0, alog, dtb, ggain, gq2, gk2):
    bn = proj3.shape[0]
    nb = HEAD_DIM
    row = lambda w: pl.BlockSpec((1, 1, w), lambda b: (b, 0, 0))
    full2 = lambda a: pl.BlockSpec(a.shape, lambda b: (0, 0))
    st = pl.BlockSpec((1, B_HEADS, nb, nb), lambda b: (b, 0, 0, 0))
    heads = pl.BlockSpec((1, C_HEADS, 1, nb), lambda b: (b, 0, 0, 0))
    sd = lambda w: jax.ShapeDtypeStruct((bn, 1, w), F32)
    hd = jax.ShapeDtypeStruct((bn, C_HEADS, 1, nb), F32)
    return pl.pallas_call(
        _sample_mix_kernel,
        grid=(bn,),
        in_specs=[row(MAIN_COLS), row(GATE_PAD),
                  pl.BlockSpec((1, CONV_WIDTH - 1, 3 * B_W), lambda b: (b, 0, 0)),
                  full2(conv_w), st, full2(w00), full2(b0), full2(alog), full2(dtb),
                  full2(ggain), full2(gq2), full2(gk2)],
        out_specs=[row(A_W), row(A_W), row(B_W), st, heads, heads, heads, heads],
        out_shape=[sd(A_W), sd(A_W), sd(B_W),
                   jax.ShapeDtypeStruct(state.shape, F32), hd, hd, hd, hd],
        compiler_params=_cparams(1),
        name="sample_mix",
    )(proj3, ab3, cbuf, conv_w, state, w00, b0, alog, dtb, ggain, gq2, gk2)


def _paged_attn_kernel(pt_ref, qn_ref, kn_ref, vn_ref, z_ref, slope_ref, lamv_ref, gain_ref, *rest,
                       lam_init, past_len, n_groups):
    del pt_ref
    n_scratch = 6
    pp = (len(rest) - 1 - n_scratch) // 2
    k_refs = rest[:pp]
    v_refs = rest[pp:2 * pp]
    o_ref = rest[2 * pp]
    sc_ref, mrun_ref, lrun_ref, acc_ref, q3_ref, snew_ref = rest[2 * pp + 1:]
    ph = pl.program_id(1)
    g = pl.program_id(2)
    page = k_refs[0].shape[1]
    sshape = mrun_ref.shape

    @pl.when((ph == 0) & (g == 0))
    def _():
        q = jnp.broadcast_to(qn_ref[0], q3_ref.shape)
        ri = lax.broadcasted_iota(jnp.int32, q3_ref.shape, 1)
        ci = lax.broadcasted_iota(jnp.int32, q3_ref.shape, 2)
        q3_ref[...] = jnp.where((ci // QK_DIM) == ri, q, 0.0)
        mrun_ref[...] = jnp.full(sshape, NEG_BIG, F32)
        lrun_ref[...] = jnp.zeros(sshape, F32)
        acc_ref[...] = jnp.zeros(acc_ref.shape, F32)

    @pl.when(ph == 0)
    def _():
        q3 = q3_ref[...].astype(BF16)
        slope = slope_ref[...]
        tok = lax.broadcasted_iota(jnp.int32, sshape, 2)
        mrun = mrun_ref[...]
        for i in range(pp):
            pg = g * pp + i
            s = _bdot_nt(q3, k_refs[i][...].astype(BF16))
            dist = (past_len - (pg * page + tok)).astype(F32)
            s = s - slope * dist
            sc_ref[pg] = s
            mrun = jnp.maximum(mrun, s)
        mrun_ref[...] = mrun

    @pl.when((ph == 1) & (g == 0))
    def _():
        s_new = jnp.sum(q3_ref[...].astype(BF16).astype(F32) * kn_ref[0], axis=-1, keepdims=True)
        m = jnp.maximum(jnp.max(mrun_ref[...], axis=-1, keepdims=True), s_new)
        mrun_ref[...] = jnp.broadcast_to(m, sshape)
        snew_ref[...] = jnp.broadcast_to(s_new, sshape)

    @pl.when(ph == 1)
    def _():
        m = mrun_ref[...]
        lrun = lrun_ref[...]
        acc = acc_ref[...]
        for i in range(pp):
            pg = g * pp + i
            p = jnp.exp(sc_ref[pg] - m)
            lrun = lrun + p
            acc = acc + _bdot(p.astype(BF16), v_refs[i][...].astype(BF16))
        lrun_ref[...] = lrun
        acc_ref[...] = acc

    @pl.when((ph == 1) & (g == n_groups - 1))
    def _():
        m = mrun_ref[:, :, 0:1]
        p_new = jnp.exp(snew_ref[:, :, 0:1] - m)
        l = jnp.sum(lrun_ref[...], axis=-1, keepdims=True) + p_new
        normed = (acc_ref[...] + p_new * vn_ref[0]) / l
        lam = _lambda_value(lamv_ref[...], lam_init)
        o = normed[:, 0:1, :] - lam * normed[:, 1:2, :]
        o = o * lax.rsqrt(jnp.mean(o * o, axis=-1, keepdims=True) + EPS) * gain_ref[...] * (1.0 - lam_init)
        o_ref[0] = o * _silu(z_ref[0])


def _paged_attn(page_table, qn, kn, vn, zc, cache_k, cache_v, layer, slope3, lamv, gain, lam_init):
    bn, n_pages = page_table.shape
    page = cache_k.shape[3]
    pp = math.gcd(PAGES_PER_STEP, n_pages)
    n_groups = n_pages // pp
    past_len = n_pages * page
    head_row = pl.BlockSpec((1, C_HEADS, 1, HEAD_DIM), lambda b, ph, g, pt: (b, 0, 0, 0))

    def k_spec(i):
        def imap(b, ph, g, pt):
            grp = jnp.where(ph == 0, g, n_groups - 1)
            return (layer, pt[b, grp * pp + i], 0, 0, 0)
        return pl.BlockSpec((None, None, C_HEADS, page, HEAD_DIM), imap)

    def v_spec(i):
        def imap(b, ph, g, pt):
            grp = jnp.where(ph == 0, 0, g)
            return (layer, pt[b, grp * pp + i], 0, 0, 0)
        return pl.BlockSpec((None, None, C_HEADS, page, HEAD_DIM), imap)

    def full(a):
        nd = a.ndim
        return pl.BlockSpec(a.shape, lambda b, ph, g, pt: (0,) * nd)

    sshape = (C_HEADS, SCORE_ROWS, page)
    qshape = (C_HEADS, SCORE_ROWS, HEAD_DIM)
    kern = functools.partial(_paged_attn_kernel, lam_init=lam_init, past_len=past_len, n_groups=n_groups)
    grid_spec = pltpu.PrefetchScalarGridSpec(
        num_scalar_prefetch=1,
        grid=(bn, 2, n_groups),
        in_specs=[head_row, head_row, head_row, head_row, full(slope3), full(lamv), full(gain)]
                 + [k_spec(i) for i in range(pp)] + [v_spec(i) for i in range(pp)],
        out_specs=head_row,
        scratch_shapes=[pltpu.VMEM((n_pages,) + sshape, F32),
                        pltpu.VMEM(sshape, F32),
                        pltpu.VMEM(sshape, F32),
                        pltpu.VMEM(qshape, F32),
                        pltpu.VMEM(qshape, F32),
                        pltpu.VMEM(sshape, F32)])
    return pl.pallas_call(
        kern,
        grid_spec=grid_spec,
        out_shape=jax.ShapeDtypeStruct((bn, C_HEADS, 1, HEAD_DIM), F32),
        compiler_params=_cparams(3),
        name="paged_attn",
    )(page_table, qn, kn, vn, zc, slope3, lamv, gain, *([cache_k] * pp), *([cache_v] * pp))


def _pad_lanes(v, width=GATE_PAD):
    return jnp.pad(v.astype(F32), (0, width - v.shape[0])).reshape(1, width)


def _layer_params(l, norm_gain, chunk_w, chunk_b, gdn_conv_w, gdn_a_log, gdn_dt_bias,
                  gdn_norm_gain, attn_q_norm, attn_k_norm, lq1, lk1, lq2, lk2, subln):
    return dict(
        layer=l, norm_gain=norm_gain[l], chunk_w=chunk_w[l],
        bias_b=jnp.broadcast_to(chunk_b[l][:, :, None], (A_GROUPS, CHUNK_MLP, HEAD_DIM)),
        w00=jnp.repeat(chunk_w[l][:, 0, 0], HEAD_DIM).reshape(1, A_W),
        b0=jnp.repeat(chunk_b[l][:, 0], HEAD_DIM).reshape(1, A_W),
        conv_w=gdn_conv_w[l], alog=_pad_lanes(gdn_a_log[l]), dtb=_pad_lanes(gdn_dt_bias[l]),
        ggain=gdn_norm_gain[l].reshape(1, HEAD_DIM),
        gq2=jnp.tile(attn_q_norm[l], 2).reshape(1, HEAD_DIM),
        gk2=jnp.tile(attn_k_norm[l], 2).reshape(1, HEAD_DIM),
        lamv=jnp.stack([lq1[l], lk1[l], lq2[l], lk2[l]]).astype(F32),
        subln=subln[l].reshape(1, HEAD_DIM),
        lam_init=0.8 - 0.6 * math.exp(-0.3 * l),
    )


MAIN_TN = 512
MAIN_TM = 1024
GATING_ROWS = 2048


def _layer(xp, xs, lp, w_in_t, w_out, qaug, kaug, depth, kv_all, slope3, state, cbuf, cache_k, cache_v,
           page_table):
    bn, t, d = xp.shape
    sn = xs.shape[0]
    m = bn * t
    xp2 = xp.reshape(m, d)
    xs2 = xs.reshape(sn, d)
    tm = min(MAIN_TM, m)
    layer = lp['layer']
    h, gates, gates_s = _norm_gates(xp2, xs2, lp['norm_gain'], w_in_t, layer, min(512, m))
    proj, proj_s = _inproj(h, xs2, lp['norm_gain'], w_in_t, layer,
                           functools.partial(_main_row_start, tn=MAIN_TN), MAIN_COLS, tm, MAIN_TN)

    ps3 = proj_s.reshape(sn, 1, MAIN_COLS)
    a_s, a_v, b_s, s_new, qn, kn_s, vn, zc = _sample_mix(
        ps3, gates_s.reshape(sn, 1, GATE_PAD), cbuf, lp['conv_w'], state, lp['w00'], lp['b0'],
        lp['alog'], lp['dtb'], lp['ggain'], lp['gq2'], lp['gk2'])
    c_s = _paged_attn(page_table, qn, kn_s, vn, zc, cache_k, cache_v, layer, slope3,
                      lp['lamv'], lp['subln'], lp['lam_init'])
    conv_s = jnp.concatenate([cbuf[:, 1:], ps3[:, :, OFF_BQ:OFF_BQ + 3 * B_W]], axis=1)

    proj3 = proj.reshape(bn, t, MAIN_COLS)
    a_out = _gating_prompt(proj, lp['chunk_w'], lp['bias_b'], min(GATING_ROWS, t))
    b_out, s_fin = _gdn_prompt(proj3, gates.reshape(bn, t, GATE_PAD), lp['conv_w'], lp['alog'], lp['dtb'],
                               lp['ggain'])
    q1, q2, kn, k1, k2, vo, vb = _cprep_prompt(proj3, lp['gq2'], lp['gk2'], qaug, kaug, min(256, t),
                                               layer, depth, kv_all)
    c_out = _attn_prompt(q1, q2, k1, k2, vb, proj3, lp['lamv'], lp['subln'], lp['lam_init'],
                         min(ATTN_BLOCK, t))
    conv_p = proj3[:, t - (CONV_WIDTH - 1):, OFF_BQ:OFF_BQ + 3 * B_W]

    yp, ys = _outproj(xp2, a_out, b_out.reshape(m, B_W), c_out.reshape(m, C_W),
                      xs2, a_s.reshape(sn, A_W), b_s.reshape(sn, B_W), c_s.reshape(sn, C_W),
                      w_out, layer, tm, MAIN_TN)
    prompt_out = (yp.reshape(bn, t, d), (kn, vo), s_fin, conv_p)
    sample_out = (ys.reshape(sn, 1, d), kn_s.reshape(sn, 1, C_HEADS, HEAD_DIM), vn.reshape(sn, 1, C_HEADS, HEAD_DIM),
                  s_new, conv_s, a_v)
    return prompt_out, sample_out


def kernel(x_prompt, x_sample, cache_attn_k, cache_attn_v, state_gdn, state_gdn_conv, page_table,
           norm_gain, w_in, w_out, chunk_w, chunk_b, gdn_conv_w, gdn_a_log, gdn_dt_bias, gdn_norm_gain,
           attn_q_norm, attn_k_norm, lambda_q1, lambda_k1, lambda_q2, lambda_k2, attn_subln_gain):
    depth = w_in.shape[0]
    slopes_np = _alibi_slopes(C_HEADS)
    page = cache_attn_k.shape[2]
    slope3 = jnp.asarray(np.broadcast_to(slopes_np[:, None, None], (C_HEADS, SCORE_ROWS, page)).copy())
    qaug, kaug = _alibi_tables(slopes_np, x_prompt.shape[1])
    w_in_t = jnp.transpose(w_in, (0, 2, 1))
    ck = jnp.transpose(cache_attn_k, (0, 1, 3, 2, 4))
    cv = jnp.transpose(cache_attn_v, (0, 1, 3, 2, 4))

    yp, ys = x_prompt, x_sample
    outs = [[] for _ in range(7)]
    kv_all = None
    for l in range(depth):
        lp = _layer_params(l, norm_gain, chunk_w, chunk_b, gdn_conv_w, gdn_a_log, gdn_dt_bias,
                           gdn_norm_gain, attn_q_norm, attn_k_norm, lambda_q1, lambda_k1, lambda_q2,
                           lambda_k2, attn_subln_gain)
        (yp, kv_all, ps, pc), (ys, sk, sv, ss, sc, sa) = _layer(
            yp, ys, lp, w_in_t, w_out, qaug, kaug, depth, kv_all, slope3, state_gdn[l], state_gdn_conv[l],
            ck, cv, page_table)
        for lst, val in zip(outs, (ps, pc, sk, sv, ss, sc, sa)):
            lst.append(val)
    res = [jnp.stack(o) for o in outs]
    pk = jnp.transpose(kv_all[0], (0, 1, 3, 2, 4))
    pv = jnp.transpose(kv_all[1], (0, 1, 3, 2, 4))
    return (yp, ys, pk, pv) + tuple(res)
```
